```python
import math
import jax
import jax.numpy as jnp
from jax import lax
import numpy as np

D_MODEL = 1024
BATCH = 8
SEQ = 2048
DEPTH = 4

GRID_W = 64
CTX_LEN = 256
N_MIXERS = 4
N_MOD = 9
D_FF = 2816
ROPE_THETA = 10000.0
Q_BLOCK = 128
NEG_INF = -1e30
EPS = 1e-6

DA_HEAD_DIM = 64
DA_HEADS = D_MODEL // (2 * DA_HEAD_DIM)
GQA_HEAD_DIM = 64
GQA_HEADS = D_MODEL // GQA_HEAD_DIM
GQA_KV_HEADS = 4
GQA_GROUP = GQA_HEADS // GQA_KV_HEADS
MLA_HEADS = 16
MLA_Q_LORA = 256
MLA_KV_LORA = 128
MLA_NOPE = 64
MLA_ROPE = 32
MLA_V = 64
NA_HEAD_DIM = 64
NA_HEADS = D_MODEL // NA_HEAD_DIM
NA_WIN_ROWS = 8
NA_WIN_COLS = 16

kernel_name = "hybrid_diffusion_trunk_prefix_ctx"


def rms_norm(x, g):
    xf = x.astype(jnp.float32)
    y = xf * lax.rsqrt(jnp.mean(xf * xf, axis=-1, keepdims=True) + EPS)
    return (y * g.astype(jnp.float32)).astype(x.dtype)


def modulate(x, g, shift, scale):
    return rms_norm(x, g) * (1 + scale) + shift


def swiglu(h, w_in, w_out):
    gate, up = jnp.split(h @ w_in, 2, axis=-1)
    return (jax.nn.silu(gate) * up) @ w_out


def axial_rope(x, row, col):
    rd = x.shape[-1]
    half = rd // 2
    nf = half // 2
    inv_freq = ROPE_THETA ** (-jnp.arange(nf, dtype=jnp.float32) / nf)

    def rot(xh, pos):
        ang = pos.astype(jnp.float32)[:, None] * inv_freq
        shape = (ang.shape[0],) + (1,) * (x.ndim - 3) + (nf,)
        cos = jnp.cos(ang).reshape(shape).astype(x.dtype)
        sin = jnp.sin(ang).reshape(shape).astype(x.dtype)
        x1, x2 = jnp.split(xh, 2, axis=-1)
        return jnp.concatenate([x1 * cos - x2 * sin, x1 * sin + x2 * cos], axis=-1)

    return jnp.concatenate([rot(x[..., :half], row), rot(x[..., half:], col)], axis=-1)


def sweep_query_blocks(attend, q):
    b, l = q.shape[:2]
    nb = l // Q_BLOCK
    qb = jnp.moveaxis(q.reshape((b, nb, Q_BLOCK) + q.shape[2:]), 1, 0)
    out = lax.map(attend, qb)
    return jnp.moveaxis(out, 0, 1).reshape((b, l) + out.shape[3:])


def gqa_attend(q, k, v, scale):
    s = jnp.einsum("bqhgd,bkhd->bhgqk", q, k).astype(jnp.float32) * scale
    p = jax.nn.softmax(s, axis=-1).astype(v.dtype)
    return jnp.einsum("bhgqk,bkhd->bqhgd", p, v)


def _da_project(t, w_qkv):
    bt, lt, _ = t.shape
    q, k, v = jnp.split(t @ w_qkv, 3, axis=-1)
    q = q.reshape(bt, lt, DA_HEADS, 2, DA_HEAD_DIM)
    k = k.reshape(bt, lt, DA_HEADS, 2, DA_HEAD_DIM)
    v = v.reshape(bt, lt, DA_HEADS, 2 * DA_HEAD_DIM)
    return q, k, v


def _diff_attend(q, k, v, lam):
    s = jnp.einsum("bqhmd,bkhmd->bhmqk", q, k).astype(jnp.float32) * DA_HEAD_DIM ** -0.5
    p = jax.nn.softmax(s, axis=-1)
    p = (p[:, :, 0] - lam * p[:, :, 1]).astype(v.dtype)
    return jnp.einsum("bhqk,bkhe->bqhe", p, v)


def diff_attention(h, hc, w_qkv, lam_q1, lam_k1, lam_q2, lam_k2, subln_g, w_o, layer_idx, row, col, with_ctx):
    b, l, _ = h.shape
    lam_init = 0.8 - 0.6 * math.exp(-0.3 * layer_idx)
    lam = (jnp.exp(jnp.sum(lam_q1 * lam_k1).astype(jnp.float32))
           - jnp.exp(jnp.sum(lam_q2 * lam_k2).astype(jnp.float32)) + lam_init)
    q, k, v = _da_project(h, w_qkv)
    qc, kc, vc = _da_project(hc, w_qkv)
    q = axial_rope(q, row, col)
    k = axial_rope(k, row, col)
    k_all = jnp.concatenate([k, kc], axis=1)
    v_all = jnp.concatenate([v, vc], axis=1)

    def finish(o):
        o = rms_norm(o, subln_g) * (1 - lam_init)
        return o.reshape(o.shape[0], o.shape[1], -1) @ w_o

    y = finish(sweep_query_blocks(lambda qb: _diff_attend(qb, k_all, v_all, lam), q))
    yc = finish(_diff_attend(qc, kc, vc, lam)) if with_ctx else None
    return y, yc


def _gqa_project(t, w_qkv, q_norm_g, k_norm_g):
    bt, lt, _ = t.shape
    q, k, v = jnp.split(t @ w_qkv, [GQA_HEADS * GQA_HEAD_DIM, (GQA_HEADS + GQA_KV_HEADS) * GQA_HEAD_DIM], axis=-1)
    q = rms_norm(q.reshape(bt, lt, GQA_KV_HEADS, GQA_GROUP, GQA_HEAD_DIM), q_norm_g)
    k = rms_norm(k.reshape(bt, lt, GQA_KV_HEADS, GQA_HEAD_DIM), k_norm_g)
    v = v.reshape(bt, lt, GQA_KV_HEADS, GQA_HEAD_DIM)
    return q, k, v


def gqa_attention(h, hc, w_qkv, q_norm_g, k_norm_g, w_o, row, col, with_ctx):
    scale = GQA_HEAD_DIM ** -0.5
    q, k, v = _gqa_project(h, w_qkv, q_norm_g, k_norm_g)
    qc, kc, vc = _gqa_project(hc, w_qkv, q_norm_g, k_norm_g)
    q = axial_rope(q, row, col)
    k = axial_rope(k, row, col)
    k_all = jnp.concatenate([k, kc], axis=1)
    v_all = jnp.concatenate([v, vc], axis=1)
    o = sweep_query_blocks(lambda qb: gqa_attend(qb, k_all, v_all, scale), q)
    y = o.reshape(o.shape[0], o.shape[1], -1) @ w_o
    yc = None
    if with_ctx:
        oc = gqa_attend(qc, kc, vc, scale)
        yc = oc.reshape(oc.shape[0], oc.shape[1], -1) @ w_o
    return y, yc


def _mla_project(t, w_down, q_norm_g, kv_norm_g, w_uq, w_ukv, row, col):
    bt, lt, _ = t.shape
    cq, ckv, k_pe = jnp.split(t @ w_down, [MLA_Q_LORA, MLA_Q_LORA + MLA_KV_LORA], axis=-1)
    q = (rms_norm(cq, q_norm_g) @ w_uq).reshape(bt, lt, MLA_HEADS, MLA_NOPE + MLA_ROPE)
    kv = (rms_norm(ckv, kv_norm_g) @ w_ukv).reshape(bt, lt, MLA_HEADS, MLA_NOPE + MLA_V)
    q_nope, q_pe = jnp.split(q, [MLA_NOPE], axis=-1)
    k_nope, v = jnp.split(kv, [MLA_NOPE], axis=-1)
    k_pe = k_pe[:, :, None, :]
    if row is not None:
        q_pe = axial_rope(q_pe, row, col)
        k_pe = axial_rope(k_pe, row, col)
    q = jnp.concatenate([q_nope, q_pe], axis=-1)[:, :, :, None, :]
    k = jnp.concatenate([k_nope, jnp.broadcast_to(k_pe, (bt, lt, MLA_HEADS, MLA_ROPE))], axis=-1)
    return q, k, v


def mla_attention(h, hc, w_down, q_norm_g, kv_norm_g, w_uq, w_ukv, w_o, row, col, with_ctx):
    scale = (MLA_NOPE + MLA_ROPE) ** -0.5
    q, k, v = _mla_project(h, w_down, q_norm_g, kv_norm_g, w_uq, w_ukv, row, col)
    qc, kc, vc = _mla_project(hc, w_down, q_norm_g, kv_norm_g, w_uq, w_ukv, None, None)
    k_all = jnp.concatenate([k, kc], axis=1)
    v_all = jnp.concatenate([v, vc], axis=1)
    o = sweep_query_blocks(lambda qb: gqa_attend(qb, k_all, v_all, scale), q)
    y = o.reshape(o.shape[0], o.shape[1], -1) @ w_o
    yc = None
    if with_ctx:
        oc = gqa_attend(qc, kc, vc, scale)
        yc = oc.reshape(oc.shape[0], oc.shape[1], -1) @ w_o
    return y, yc


def _na_project(t, w_qkv):
    bt, lt, _ = t.shape
    q, k, v = jnp.split(t @ w_qkv, 3, axis=-1)
    shp = (bt, lt, NA_HEADS, NA_HEAD_DIM)
    return q.reshape(shp), k.reshape(shp), v.reshape(shp)


def neighbourhood_attention(h, hc, w_qkv, rpb, w_o, with_ctx):
    b, l, _ = h.shape
    rows = l // GRID_W
    wr = min(NA_WIN_ROWS, rows)
    scale = NA_HEAD_DIM ** -0.5
    q, k, v = _na_project(h, w_qkv)
    qc, kc, vc = _na_project(hc, w_qkv)
    q = q.reshape(b, rows, GRID_W, NA_HEADS, NA_HEAD_DIM)
    k = k.reshape(b, rows, GRID_W, NA_HEADS, NA_HEAD_DIM)
    v = v.reshape(b, rows, GRID_W, NA_HEADS, NA_HEAD_DIM)

    cols = jnp.arange(GRID_W)
    col_start = jnp.clip(cols - NA_WIN_COLS // 2, 0, GRID_W - NA_WIN_COLS)
    col_in = (cols[None, :] >= col_start[:, None]) & (cols[None, :] < col_start[:, None] + NA_WIN_COLS)
    col_idx = jnp.clip(cols[None, :] - cols[:, None] + NA_WIN_COLS - 1, 0, 2 * NA_WIN_COLS - 2)
    row_ids = jnp.arange(rows)
    row_start = jnp.clip(row_ids - wr // 2, 0, rows - wr)

    def attend_row(args):
        q_r, r, rs = args
        k_band = lax.dynamic_slice_in_dim(k, rs, wr, axis=1)
        v_band = lax.dynamic_slice_in_dim(v, rs, wr, axis=1)
        row_idx = rs + jnp.arange(wr) - r + (NA_WIN_ROWS - 1)
        bias = jnp.transpose(rpb[:, row_idx][:, :, col_idx], (0, 2, 1, 3)).astype(jnp.float32)
        s_nb = jnp.einsum("bqhd,bjkhd->bhqjk", q_r, k_band).astype(jnp.float32) * scale + bias
        s_nb = jnp.where(col_in[:, None, :], s_nb, NEG_INF).reshape(b, NA_HEADS, GRID_W, wr * GRID_W)
        s_cx = jnp.einsum("bqhd,bchd->bhqc", q_r, kc).astype(jnp.float32) * scale
        p = jax.nn.softmax(jnp.concatenate([s_nb, s_cx], axis=-1), axis=-1).astype(v.dtype)
        p_nb = p[..., :wr * GRID_W].reshape(b, NA_HEADS, GRID_W, wr, GRID_W)
        p_cx = p[..., wr * GRID_W:]
        return (jnp.einsum("bhqjk,bjkhd->bqhd", p_nb, v_band)
                + jnp.einsum("bhqc,bchd->bqhd", p_cx, vc))

    o = lax.map(attend_row, (jnp.moveaxis(q, 1, 0), row_ids, row_start))
    y = jnp.moveaxis(o, 0, 1).reshape(b, l, NA_HEADS * NA_HEAD_DIM) @ w_o
    yc = None
    if with_ctx:
        oc = gqa_attend(qc[:, :, :, None, :], kc, vc, scale)
        yc = oc.reshape(oc.shape[0], oc.shape[1], -1) @ w_o
    return y, yc


def setup_inputs(seed: int = 0) -> dict:
    key = jax.random.key(seed)
    keys = iter(jax.random.split(key, 32))
    f32 = jnp.float32
    D = D_MODEL

    def normal(shape, std):
        return jax.random.normal(next(keys), shape, f32) * std

    def gain(shape):
        return 1.0 + normal(shape, 0.05)

    n_a, n_b, n_c, n_d = [len(range(m, DEPTH, N_MIXERS)) for m in range(N_MIXERS)]
    da_width = DA_HEADS * 2 * DA_HEAD_DIM
    gqa_width = GQA_HEADS * GQA_HEAD_DIM
    na_width = NA_HEADS * NA_HEAD_DIM
    return {
        "x": normal((BATCH, SEQ, D), 1.0),
        "c": normal((BATCH, D), 1.0),
        "ctx": normal((BATCH, CTX_LEN, D), 1.0),
        "c_ctx": normal((D,), 1.0),
        "w_mod": normal((DEPTH, D, N_MOD * D), 0.5 * D ** -0.5),
        "b_mod": normal((DEPTH, N_MOD * D), 0.02),
        "norm_g": gain((DEPTH, 3, D)),
        "w_ffn_in": normal((DEPTH, 2, D, 2 * D_FF), D ** -0.5),
        "w_ffn_out": normal((DEPTH, 2, D_FF, D), D_FF ** -0.5),
        "da_w_qkv": normal((n_a, D, 3 * da_width), D ** -0.5),
        "da_lam_q1": normal((n_a, DA_HEAD_DIM), 0.1),
        "da_lam_k1": normal((n_a, DA_HEAD_DIM), 0.1),
        "da_lam_q2": normal((n_a, DA_HEAD_DIM), 0.1),
        "da_lam_k2": normal((n_a, DA_HEAD_DIM), 0.1),
        "da_subln_g": gain((n_a, 2 * DA_HEAD_DIM)),
        "da_w_o": normal((n_a, da_width, D), da_width ** -0.5),
        "gqa_w_qkv": normal((n_b, D, (GQA_HEADS + 2 * GQA_KV_HEADS) * GQA_HEAD_DIM), D ** -0.5),
        "gqa_q_norm_g": gain((n_b, GQA_HEAD_DIM)),
        "gqa_k_norm_g": gain((n_b, GQA_HEAD_DIM)),
        "gqa_w_o": normal((n_b, gqa_width, D), gqa_width ** -0.5),
        "mla_w_down": normal((n_c, D, MLA_Q_LORA + MLA_KV_LORA + MLA_ROPE), D ** -0.5),
        "mla_q_norm_g": gain((n_c, MLA_Q_LORA)),
        "mla_kv_norm_g": gain((n_c, MLA_KV_LORA)),
        "mla_w_uq": normal((n_c, MLA_Q_LORA, MLA_HEADS * (MLA_NOPE + MLA_ROPE)), MLA_Q_LORA ** -0.5),
        "mla_w_ukv": normal((n_c, MLA_KV_LORA, MLA_HEADS * (MLA_NOPE + MLA_V)), MLA_KV_LORA ** -0.5),
        "mla_w_o": normal((n_c, MLA_HEADS * MLA_V, D), (MLA_HEADS * MLA_V) ** -0.5),
        "na_w_qkv": normal((n_d, D, 3 * na_width), D ** -0.5),
        "na_rpb": normal((n_d, NA_HEADS, 2 * NA_WIN_ROWS - 1, 2 * NA_WIN_COLS - 1), 0.1),
        "na_w_o": normal((n_d, na_width, D), na_width ** -0.5),
        "final_g": gain((D,)),
    }


def reference(x, c, ctx, c_ctx, w_mod, b_mod, norm_g, w_ffn_in, w_ffn_out,
              da_w_qkv, da_lam_q1, da_lam_k1, da_lam_q2, da_lam_k2, da_subln_g, da_w_o,
              gqa_w_qkv, gqa_q_norm_g, gqa_k_norm_g, gqa_w_o,
              mla_w_down, mla_q_norm_g, mla_kv_norm_g, mla_w_uq, mla_w_ukv, mla_w_o,
              na_w_qkv, na_rpb, na_w_o, final_g):
    b, l, d = x.shape
    t = jnp.arange(l)
    row = t // GRID_W
    col = t % GRID_W
    silu_c = jax.nn.silu(c)
    silu_cc = jax.nn.silu(c_ctx)
    xc = ctx
    for i in range(DEPTH):
        kind, inst = i % N_MIXERS, i // N_MIXERS
        with_ctx = i < DEPTH - 1
        mod = (silu_c @ w_mod[i] + b_mod[i]).reshape(b, N_MOD, 1, d)
        mod_c = (silu_cc @ w_mod[i] + b_mod[i]).reshape(N_MOD, d)

        x = x + 0.5 * mod[:, 2] * swiglu(modulate(x, norm_g[i, 0], mod[:, 0], mod[:, 1]), w_ffn_in[i, 0], w_ffn_out[i, 0])
        xc = xc + 0.5 * mod_c[2] * swiglu(modulate(xc, norm_g[i, 0], mod_c[0], mod_c[1]), w_ffn_in[i, 0], w_ffn_out[i, 0])

        h = modulate(x, norm_g[i, 1], mod[:, 3], mod[:, 4])
        hc = modulate(xc, norm_g[i, 1], mod_c[3], mod_c[4])
        if kind == 0:
            y, yc = diff_attention(h, hc, da_w_qkv[inst], da_lam_q1[inst], da_lam_k1[inst], da_lam_q2[inst],
                                   da_lam_k2[inst], da_subln_g[inst], da_w_o[inst], i, row, col, with_ctx)
        elif kind == 1:
            y, yc = gqa_attention(h, hc, gqa_w_qkv[inst], gqa_q_norm_g[inst], gqa_k_norm_g[inst], gqa_w_o[inst],
                                  row, col, with_ctx)
        elif kind == 2:
            y, yc = mla_attention(h, hc, mla_w_down[inst], mla_q_norm_g[inst], mla_kv_norm_g[inst], mla_w_uq[inst],
                                  mla_w_ukv[inst], mla_w_o[inst], row, col, with_ctx)
        else:
            y, yc = neighbourhood_attention(h, hc, na_w_qkv[inst], na_rpb[inst], na_w_o[inst], with_ctx)
        x = x + mod[:, 5] * y

        x = x + 0.5 * mod[:, 8] * swiglu(modulate(x, norm_g[i, 2], mod[:, 6], mod[:, 7]), w_ffn_in[i, 1], w_ffn_out[i, 1])
        if with_ctx:
            xc = xc + mod_c[5] * yc
            xc = xc + 0.5 * mod_c[8] * swiglu(modulate(xc, norm_g[i, 2], mod_c[6], mod_c[7]), w_ffn_in[i, 1], w_ffn_out[i, 1])
    return rms_norm(x, final_g)
```

```python
import functools
import math

import jax
import jax.numpy as jnp
from jax import lax
from jax.experimental import pallas as pl
from jax.experimental.pallas import tpu as pltpu

F32 = jnp.float32
BF16 = jnp.bfloat16

D = 1024
BATCH = 8
SEQ = 2048
DEPTH = 4
GRID_W = 64
GRID_ROWS = SEQ // GRID_W
CTX = 256
N_MOD = 9
D_FF = 2816
ROPE_THETA = 10000.0
NEG_INF = -1e30
EPS = 1e-6

N_LAT = BATCH * SEQ
N_TOK = N_LAT + BATCH * CTX
N_GROUPS = N_TOK // SEQ
MOD_ROWS = 16
assert BATCH * CTX == SEQ and N_GROUPS == BATCH + 1 and N_GROUPS <= MOD_ROWS

LANES = 128
MXU_N = 256
HEAD = 64
FF_CHUNKS = D_FF // MXU_N
assert FF_CHUNKS * MXU_N == D_FF

DA_HEADS = 8
GQA_HEADS, GQA_KV = 16, 4
MLA_HEADS, MLA_Q_LORA, MLA_KV_LORA, MLA_NOPE, MLA_ROPE = 16, 256, 128, 64, 32
NA_HEADS, NA_WIN_ROWS, NA_WIN_COLS = 16, 8, 16
NA_BAND = NA_WIN_ROWS * GRID_W
NA_VARIANTS = NA_WIN_ROWS

TM = 512
TQ = 256
MOD_TN = 1024
LAT_TILES = SEQ // TM
QT_PER_BATCH = SEQ // TQ
assert TQ == CTX
VMEM_LIMIT = 56 * 1024 * 1024

NT = (((1,), (1,)), ((), ()))


def _params(*sem):
    return pltpu.CompilerParams(dimension_semantics=sem, vmem_limit_bytes=VMEM_LIMIT)


def _resident(shape, index_map):
    return pl.BlockSpec(shape, index_map, pipeline_mode=pl.Buffered(1))


def _bdot(a, b):
    return jnp.dot(a, b, preferred_element_type=F32)


def _rms(x):
    return x * lax.rsqrt(jnp.mean(x * x, axis=-1, keepdims=True) + EPS)


def _modulated(x, g, shift, scale):
    return (_rms(x) * g) * (1.0 + scale) + shift


def _mod_kernel(c_ref, w_ref, b_ref, o_ref):
    c = c_ref[...]
    s = (c * jax.nn.sigmoid(c)).astype(BF16)
    o_ref[...] = _bdot(s, w_ref[...].astype(BF16)) + b_ref[...]


def _modulation(cvec, w_mod, b_mod):
    n = N_MOD * D
    return pl.pallas_call(
        _mod_kernel,
        grid=(DEPTH, n // MOD_TN),
        in_specs=[pl.BlockSpec((MOD_ROWS, D), lambda l, j: (0, 0)),
                  pl.BlockSpec((None, D, MOD_TN), lambda l, j: (l, 0, j)),
                  pl.BlockSpec((None, 1, MOD_TN), lambda l, j: (l, 0, j))],
        out_specs=pl.BlockSpec((None, MOD_ROWS, MOD_TN), lambda l, j: (l, 0, j)),
        out_shape=jax.ShapeDtypeStruct((DEPTH, MOD_ROWS, n), F32),
        compiler_params=_params("arbitrary", "arbitrary"),
        name="modulation",
    )(cvec, w_mod, b_mod.reshape(DEPTH, 1, n))


def _ffn_kernel(*refs, mod_row, has_attn, final_norm):
    it = iter(refs)
    x_ref, mod_ref, g_ref, win_ref, wout_ref = (next(it) for _ in range(5))
    o_ref, wo_ref = (next(it), next(it)) if has_attn else (None, None)
    fg_ref = next(it) if final_norm else None
    out_ref, a_scr = next(it), next(it)

    x = x_ref[...]
    mod = mod_ref[...]
    if has_attn:
        x = x + mod[5:6] * _bdot(o_ref[...], wo_ref[...])
    shift, scale, gate = (mod[mod_row + k:mod_row + k + 1] for k in range(3))
    h = _modulated(x, g_ref[...], shift, scale).astype(BF16)
    for c in range(FF_CHUNKS):
        lo = c * MXU_N
        gt = _bdot(h, win_ref[:, lo:lo + MXU_N])
        up = _bdot(h, win_ref[:, D_FF + lo:D_FF + lo + MXU_N])
        a_scr[:, lo:lo + MXU_N] = ((gt * jax.nn.sigmoid(gt)) * up).astype(BF16)
    y = x + (0.5 * gate) * _bdot(a_scr[...], wout_ref[...])
    if final_norm:
        y = _rms(y) * fg_ref[...]
    out_ref[...] = y


def _ffn(x, mod, layer, g, w_in, w_out, *, mod_row, n_rows, attn=None, final_g=None):
    tok = lambda i: (i, 0)
    const = lambda i: (0, 0)
    in_specs = [pl.BlockSpec((TM, D), tok),
                pl.BlockSpec((None, None, N_MOD, D), lambda i: (layer, i // LAT_TILES, 0, 0)),
                _resident((1, D), const),
                _resident((D, 2 * D_FF), const),
                _resident((D_FF, D), const)]
    args = [x, mod, g.reshape(1, D), w_in, w_out]
    if attn is not None:
        o, w_o = attn
        in_specs += [pl.BlockSpec((TM, D), tok), _resident((D, D), const)]
        args += [o, w_o]
    if final_g is not None:
        in_specs.append(_resident((1, D), const))
        args.append(final_g.reshape(1, D))
    return pl.pallas_call(
        functools.partial(_ffn_kernel, mod_row=mod_row, has_attn=attn is not None,
                          final_norm=final_g is not None),
        grid=(n_rows // TM,),
        in_specs=in_specs,
        out_specs=pl.BlockSpec((TM, D), tok),
        out_shape=jax.ShapeDtypeStruct((n_rows, D), F32),
        scratch_shapes=[pltpu.VMEM((TM, D_FF), BF16)],
        compiler_params=_params("arbitrary"),
        name="ffn",
    )(*args)


def _rope_tables(rd, lane0, period):
    t = jnp.arange(SEQ)
    half, nf = rd // 2, rd // 4
    inv_freq = ROPE_THETA ** (-jnp.arange(nf, dtype=F32) / nf)
    ang_r = (t // GRID_W).astype(F32)[:, None] * inv_freq
    ang_c = (t % GRID_W).astype(F32)[:, None] * inv_freq
    zero = jnp.zeros((SEQ, nf), F32)
    cos = jnp.concatenate([jnp.cos(ang_r)] * 2 + [jnp.cos(ang_c)] * 2, axis=1)
    s_up = jnp.concatenate([-jnp.sin(ang_r), zero, -jnp.sin(ang_c), zero], axis=1)
    s_dn = jnp.concatenate([zero, jnp.sin(ang_r), zero, jnp.sin(ang_c)], axis=1)
    assert half == 2 * nf and LANES % period == 0 and lane0 + rd <= period

    def place(blk, fill):
        pat = jnp.full((SEQ, period), fill, F32).at[:, lane0:lane0 + rd].set(blk)
        pat = jnp.tile(pat, (1, LANES // period))
        return jnp.concatenate([pat, jnp.full((TM, LANES), fill, F32)], axis=0)

    return place(cos, 1.0), place(s_up, 0.0), place(s_dn, 0.0)


def _rope(x, cos, s_up, s_dn, q):
    return x * cos + pltpu.roll(x, LANES - q, 1) * s_up + pltpu.roll(x, q, 1) * s_dn


def _rope_specs():
    m = lambda i: (jnp.where(i < BATCH * LAT_TILES, i % LAT_TILES, LAT_TILES), 0)
    return [pl.BlockSpec((TM, LANES), m)] * 3


def _blocks(n_cols):
    return range(0, n_cols, LANES)


def _head_sumsq(x, ones_blk):
    xx = x * x
    hi = xx.astype(BF16)
    lo = (xx - hi.astype(F32)).astype(BF16)
    return _bdot(hi, ones_blk) + _bdot(lo, ones_blk)


def _mixer_input(x_ref, mod_ref, g_ref):
    mod = mod_ref[...]
    return _modulated(x_ref[...], g_ref[...], mod[3:4], mod[4:5]).astype(BF16)


def _qk_tables(c_ref, su_ref, sd_ref):
    k_tabs = (c_ref[...], su_ref[...], sd_ref[...])
    return tuple(t * HEAD ** -0.5 for t in k_tabs), k_tabs


def _da_proj_kernel(x_ref, mod_ref, g_ref, w_ref, c_ref, su_ref, sd_ref, out_ref):
    h = _mixer_input(x_ref, mod_ref, g_ref)
    q_tabs, k_tabs = _qk_tables(c_ref, su_ref, sd_ref)
    for lo in _blocks(2 * D):
        y = _bdot(h, w_ref[:, lo:lo + LANES])
        out_ref[:, lo:lo + LANES] = _rope(y, *(q_tabs if lo < D else k_tabs), HEAD // 4).astype(BF16)
    out_ref[:, 2 * D:] = _bdot(h, w_ref[:, 2 * D:]).astype(BF16)


def _gqa_proj_kernel(x_ref, mod_ref, g_ref, w_ref, c_ref, su_ref, sd_ref, qg_ref, kg_ref, ones_ref, out_ref):
    h = _mixer_input(x_ref, mod_ref, g_ref)
    q_tabs, k_tabs = _qk_tables(c_ref, su_ref, sd_ref)
    ones_blk = ones_ref[...]
    n_qk = D + GQA_KV * LANES
    for lo in _blocks(n_qk):
        y = _bdot(h, w_ref[:, lo:lo + LANES])
        gain = qg_ref[...] if lo < D else kg_ref[...]
        y = (y * lax.rsqrt(_head_sumsq(y, ones_blk) * (1.0 / HEAD) + EPS)) * gain
        out_ref[:, lo:lo + LANES] = _rope(y, *(q_tabs if lo < D else k_tabs), HEAD // 4).astype(BF16)
    out_ref[:, n_qk:] = _bdot(h, w_ref[:, n_qk:]).astype(BF16)


def _mla_proj_kernel(x_ref, mod_ref, g_ref, wd_ref, c_ref, su_ref, sd_ref, qg_ref, kvg_ref,
                     wuq_ref, wukv_ref, out_ref):
    h = _mixer_input(x_ref, mod_ref, g_ref)
    c, su, sd = c_ref[...], su_ref[...], sd_ref[...]
    d = _bdot(h, wd_ref[...])
    cq = (_rms(d[:, :MLA_Q_LORA]) * qg_ref[...]).astype(BF16)
    ckv = (_rms(d[:, MLA_Q_LORA:MLA_Q_LORA + MLA_KV_LORA]) * kvg_ref[...]).astype(BF16)
    k_pe = _rope(d[:, MLA_Q_LORA + MLA_KV_LORA:], c, su, sd, MLA_ROPE // 4)
    n_q = MLA_HEADS * LANES
    for lo in _blocks(n_q):
        q = _bdot(cq, wuq_ref[:, lo:lo + LANES])
        out_ref[:, lo:lo + LANES] = _rope(q, c, su, sd, MLA_ROPE // 4).astype(BF16)
        k = _bdot(ckv, wukv_ref[:, lo:lo + LANES])
        out_ref[:, n_q + lo:n_q + lo + LANES] = (k + k_pe).astype(BF16)
    out_ref[:, 2 * n_q:] = _bdot(ckv, wukv_ref[:, n_q:]).astype(BF16)


def _na_proj_kernel(x_ref, mod_ref, g_ref, w_ref, out_ref):
    h = _mixer_input(x_ref, mod_ref, g_ref)
    out_ref[:, :D] = (_bdot(h, w_ref[:, :D]) * HEAD ** -0.5).astype(BF16)
    out_ref[:, D:] = _bdot(h, w_ref[:, D:]).astype(BF16)


def _project(kernel, x, mod, layer, g, w, extra, extra_specs, out_cols, name):
    const = lambda i: (0, 0)
    in_specs = [pl.BlockSpec((TM, D), lambda i: (i, 0)),
                pl.BlockSpec((None, None, N_MOD, D), lambda i: (layer, i // LAT_TILES, 0, 0)),
                _resident((1, D), const),
                _resident(w.shape, const)] + extra_specs
    return pl.pallas_call(
        kernel,
        grid=(N_TOK // TM,),
        in_specs=in_specs,
        out_specs=pl.BlockSpec((TM, out_cols), lambda i: (i, 0)),
        out_shape=jax.ShapeDtypeStruct((N_TOK, out_cols), BF16),
        compiler_params=_params("arbitrary"),
        name=name,
    )(x, mod, g.reshape(1, D), w, *extra)


def _row(v):
    return v.reshape(1, -1).astype(F32)


def _row_spec(n):
    return _resident((1, n), lambda i: (0, 0))


def _lane_masks():
    lane = lax.broadcasted_iota(jnp.int32, (1, LANES), 1)
    first = lane < HEAD
    return first, first.astype(F32), (~first).astype(F32)


def _attend(q, segs, scale=None):
    ss = [lax.dot_general(q, k, NT, preferred_element_type=F32) for k, _ in segs]
    if scale is not None:
        ss = [s * scale for s in ss]
    m = functools.reduce(jnp.maximum, [jnp.max(s, axis=-1, keepdims=True) for s in ss])
    es = [jnp.exp(s - m) for s in ss]
    l = functools.reduce(jnp.add, [jnp.sum(e, axis=-1, keepdims=True) for e in es])
    o = functools.reduce(jnp.add, [_bdot(e.astype(BF16), v) for e, (_, v) in zip(es, segs)])
    return o / l


def _on_tile_kind(run, kl_ref, kc_ref, vl_ref, vc_ref):
    t = pl.program_id(2)

    @pl.when(t < QT_PER_BATCH)
    def _():
        run([(kl_ref, vl_ref), (kc_ref, vc_ref)])

    @pl.when(t == QT_PER_BATCH)
    def _():
        run([(kc_ref, vc_ref)])


def _da_attn_kernel(q_ref, kl_ref, kc_ref, vl_ref, vc_ref, lam_ref, g_ref, o_ref, *, lam_init):
    first, m0, m1 = _lane_masks()
    q = q_ref[...].astype(F32)
    q1, q2 = (q * m0).astype(BF16), (q * m1).astype(BF16)
    lv = lam_ref[...]
    lam = (jnp.exp(jnp.sum(lv[0:1] * lv[1:2], axis=-1, keepdims=True))
           - jnp.exp(jnp.sum(lv[2:3] * lv[3:4], axis=-1, keepdims=True)) + lam_init)

    def run(seg_refs):
        segs = [(k[...], v[...]) for k, v in seg_refs]
        o = _attend(q1, segs) - lam * _attend(q2, segs)
        o_ref[...] = ((_rms(o) * g_ref[...]) * (1.0 - lam_init)).astype(BF16)

    _on_tile_kind(run, kl_ref, kc_ref, vl_ref, vc_ref)


def _pair_attn_kernel(q_ref, kl_ref, kc_ref, vl_ref, vc_ref, o_ref, *, n_pairs, k_per_head, scale):
    first, m0, m1 = _lane_masks()

    def run(seg_refs):
        for p in range(n_pairs):
            outs = []
            for j in range(2):
                if k_per_head:
                    lo = (2 * p + j) * LANES
                    qj = q_ref[:, lo:lo + LANES]
                    segs = [(k[:, lo:lo + LANES], v[...]) for k, v in seg_refs]
                else:
                    qj = (q_ref[:, p * LANES:(p + 1) * LANES].astype(F32) * (m0, m1)[j]).astype(BF16)
                    segs = [(k[...], v[...]) for k, v in seg_refs]
                outs.append(_attend(qj, segs, scale))
            o_ref[:, p * LANES:(p + 1) * LANES] = jnp.where(first, outs[0], outs[1]).astype(BF16)

    _on_tile_kind(run, kl_ref, kc_ref, vl_ref, vc_ref)


def _attention(kernel, qkv, q_cols, k_cols, v_cols, o_cols, k0, v0, n_hb, extra=(), extra_specs=(), name=""):
    qrow = lambda b, t: jnp.where(t < QT_PER_BATCH, b * QT_PER_BATCH + t, N_LAT // TQ + b)
    kb, vb = k0 // k_cols, v0 // v_cols
    in_specs = [pl.BlockSpec((TQ, q_cols), lambda b, h, t: (qrow(b, t), h)),
                pl.BlockSpec((SEQ, k_cols), lambda b, h, t: (b, kb + h)),
                pl.BlockSpec((CTX, k_cols), lambda b, h, t: (N_LAT // CTX + b, kb + h)),
                pl.BlockSpec((SEQ, v_cols), lambda b, h, t: (b, vb + h)),
                pl.BlockSpec((CTX, v_cols), lambda b, h, t: (N_LAT // CTX + b, vb + h))] + list(extra_specs)
    return pl.pallas_call(
        kernel,
        grid=(BATCH, n_hb, QT_PER_BATCH + 1),
        in_specs=in_specs,
        out_specs=pl.BlockSpec((TQ, o_cols), lambda b, h, t: (qrow(b, t), h)),
        out_shape=jax.ShapeDtypeStruct((N_TOK, D), BF16),
        compiler_params=_params("arbitrary", "arbitrary", "arbitrary"),
        name=name,
    )(qkv, qkv, qkv, qkv, qkv, *extra)


def _na_attn_kernel(q_ref, kl_ref, kc_ref, vl_ref, vc_ref, bias_ref, o_ref):
    first, m0, m1 = _lane_masks()
    kc, vc = kc_ref[...], vc_ref[...]

    def body(r, carry):
        rs = jnp.clip(r - NA_WIN_ROWS // 2, 0, GRID_ROWS - NA_WIN_ROWS)
        q0 = pl.multiple_of(r * GRID_W, GRID_W)
        b0 = pl.multiple_of(rs * GRID_W, GRID_W)
        q = q_ref[pl.ds(q0, GRID_W), :].astype(F32)
        kb, vb = kl_ref[pl.ds(b0, NA_BAND), :], vl_ref[pl.ds(b0, NA_BAND), :]
        outs = []
        for j in range(2):
            qj = (q * (m0, m1)[j]).astype(BF16)
            s_nb = lax.dot_general(qj, kb, NT, preferred_element_type=F32) + bias_ref[r - rs, j]
            s_cx = lax.dot_general(qj, kc, NT, preferred_element_type=F32)
            m = jnp.maximum(jnp.max(s_nb, axis=-1, keepdims=True), jnp.max(s_cx, axis=-1, keepdims=True))
            e_nb, e_cx = jnp.exp(s_nb - m), jnp.exp(s_cx - m)
            l = jnp.sum(e_nb, axis=-1, keepdims=True) + jnp.sum(e_cx, axis=-1, keepdims=True)
            outs.append((_bdot(e_nb.astype(BF16), vb) + _bdot(e_cx.astype(BF16), vc)) / l)
        o_ref[pl.ds(q0, GRID_W), :] = jnp.where(first, outs[0], outs[1]).astype(BF16)
        return carry

    lax.fori_loop(0, GRID_ROWS, body, 0)


def _na_bias(rpb):
    cols = jnp.arange(GRID_W)
    col_start = jnp.clip(cols - NA_WIN_COLS // 2, 0, GRID_W - NA_WIN_COLS)
    col_in = (cols[None, :] >= col_start[:, None]) & (cols[None, :] < col_start[:, None] + NA_WIN_COLS)
    col_idx = jnp.clip(cols[None, :] - cols[:, None] + NA_WIN_COLS - 1, 0, 2 * NA_WIN_COLS - 2)
    row_idx = jnp.arange(NA_WIN_ROWS)[None, :] - jnp.arange(NA_VARIANTS)[:, None] + (NA_WIN_ROWS - 1)
    b = rpb[:, row_idx][:, :, :, col_idx]
    b = jnp.where(col_in[None, None, None], b, NEG_INF)
    b = jnp.transpose(b, (1, 0, 3, 2, 4))
    return b.reshape(NA_VARIANTS, NA_HEADS, GRID_W, NA_BAND).astype(F32)


def _na_attention(qkv, bias):
    n_hb = NA_HEADS // 2
    kb, vb = D // LANES, 2 * D // LANES
    in_specs = [pl.BlockSpec((SEQ, LANES), lambda b, h: (b, h)),
                pl.BlockSpec((SEQ, LANES), lambda b, h: (b, kb + h)),
                pl.BlockSpec((CTX, LANES), lambda b, h: (N_LAT // CTX + b, kb + h)),
                pl.BlockSpec((SEQ, LANES), lambda b, h: (b, vb + h)),
                pl.BlockSpec((CTX, LANES), lambda b, h: (N_LAT // CTX + b, vb + h)),
                pl.BlockSpec((NA_VARIANTS, 2, GRID_W, NA_BAND), lambda b, h: (0, h, 0, 0))]
    return pl.pallas_call(
        _na_attn_kernel,
        grid=(BATCH, n_hb),
        in_specs=in_specs,
        out_specs=pl.BlockSpec((SEQ, LANES), lambda b, h: (b, h)),
        out_shape=jax.ShapeDtypeStruct((N_LAT, D), BF16),
        compiler_params=_params("arbitrary", "arbitrary"),
        name="na_attention",
    )(qkv, qkv, qkv, qkv, qkv, bias)


def _dup_heads(w, n_heads):
    w = w.reshape(D, n_heads, 1, HEAD)
    return jnp.broadcast_to(w, (D, n_heads, LANES // HEAD, HEAD)).reshape(D, n_heads * LANES)


def _pad_heads(w, width):
    k = w.shape[0]
    w = w.reshape(k, MLA_HEADS, width)
    return jnp.pad(w, ((0, 0), (0, 0), (0, LANES - width))).reshape(k, MLA_HEADS * LANES)


def kernel(x, c, ctx, c_ctx, w_mod, b_mod, norm_g, w_ffn_in, w_ffn_out, da_w_qkv, da_lam_q1, da_lam_k1, da_lam_q2, da_lam_k2, da_subln_g, da_w_o, gqa_w_qkv, gqa_q_norm_g, gqa_k_norm_g, gqa_w_o, mla_w_down, mla_q_norm_g, mla_kv_norm_g, mla_w_uq, mla_w_ukv, mla_w_o, na_w_qkv, na_rpb, na_w_o, final_g):
    assert x.shape == (BATCH, SEQ, D) and ctx.shape == (BATCH, CTX, D) and w_mod.shape[0] == DEPTH == 4

    cvec = jnp.zeros((MOD_ROWS, D), F32).at[:BATCH].set(c).at[BATCH].set(c_ctx)
    mod = _modulation(cvec, w_mod, b_mod).reshape(DEPTH, MOD_ROWS, N_MOD, D)
    xs = jnp.concatenate([x.reshape(N_LAT, D), ctx.reshape(BATCH * CTX, D)], axis=0)
    w_in, w_out = w_ffn_in.astype(BF16), w_ffn_out.astype(BF16)

    rope64 = _rope_tables(HEAD, 0, HEAD)
    rope_mla = _rope_tables(MLA_ROPE, MLA_NOPE, LANES)
    ones_blk = jnp.kron(jnp.eye(LANES // HEAD, dtype=F32), jnp.ones((HEAD, HEAD), F32)).astype(BF16)
    blk_spec = _resident((LANES, LANES), lambda i: (0, 0))

    for layer in range(DEPTH):
        last = layer == DEPTH - 1
        xs = _ffn(xs, mod, layer, norm_g[layer, 0], w_in[layer, 0], w_out[layer, 0], mod_row=0, n_rows=N_TOK)
        g_mix = norm_g[layer, 1]
        if layer == 0:
            qkv = _project(_da_proj_kernel, xs, mod, layer, g_mix, da_w_qkv[0].astype(BF16),
                           list(rope64), _rope_specs(), 3 * D, "da_proj")
            lam_init = 0.8 - 0.6 * math.exp(-0.3 * layer)
            lam = jnp.zeros((8, LANES), F32).at[:4, :HEAD].set(
                jnp.stack([da_lam_q1[0], da_lam_k1[0], da_lam_q2[0], da_lam_k2[0]]))
            const3 = lambda b, h, t: (0, 0)
            o = _attention(functools.partial(_da_attn_kernel, lam_init=lam_init), qkv,
                           LANES, LANES, LANES, LANES, D, 2 * D, DA_HEADS,
                           extra=(lam, _row(da_subln_g[0])),
                           extra_specs=(pl.BlockSpec((8, LANES), const3), pl.BlockSpec((1, LANES), const3)),
                           name="da_attention")
            w_o = da_w_o[0]
        elif layer == 1:
            wq, wk, wv = jnp.split(gqa_w_qkv[0], [GQA_HEADS * HEAD, (GQA_HEADS + GQA_KV) * HEAD], axis=1)
            w = jnp.concatenate([wq, _dup_heads(wk, GQA_KV), _dup_heads(wv, GQA_KV)], axis=1).astype(BF16)
            gains = (_row(jnp.tile(gqa_q_norm_g[0], LANES // HEAD)), _row(jnp.tile(gqa_k_norm_g[0], LANES // HEAD)))
            qkv = _project(_gqa_proj_kernel, xs, mod, layer, g_mix, w,
                           list(rope64) + list(gains) + [ones_blk],
                           _rope_specs() + [_row_spec(LANES)] * 2 + [blk_spec],
                           D + 2 * GQA_KV * LANES, "gqa_proj")
            o = _attention(functools.partial(_pair_attn_kernel, n_pairs=2, k_per_head=False, scale=None), qkv,
                           2 * LANES, LANES, LANES, 2 * LANES, D, D + GQA_KV * LANES, GQA_KV,
                           name="gqa_attention")
            w_o = gqa_w_o[0]
        elif layer == 2:
            wd = mla_w_down[0]
            n_lora = MLA_Q_LORA + MLA_KV_LORA
            wd = jnp.concatenate([wd[:, :n_lora], jnp.zeros((D, MLA_NOPE), F32), wd[:, n_lora:],
                                  jnp.zeros((D, LANES - MLA_NOPE - MLA_ROPE), F32)], axis=1).astype(BF16)
            wuq = _pad_heads(mla_w_uq[0], MLA_NOPE + MLA_ROPE).astype(BF16)
            wukv = mla_w_ukv[0].reshape(MLA_KV_LORA, MLA_HEADS, 2 * HEAD)
            wuk = _pad_heads(wukv[:, :, :MLA_NOPE].reshape(MLA_KV_LORA, -1), MLA_NOPE)
            wuv = wukv[:, :, MLA_NOPE:].reshape(MLA_KV_LORA, MLA_HEADS * HEAD)
            wukv = jnp.concatenate([wuk, wuv], axis=1).astype(BF16)
            n_q = MLA_HEADS * LANES
            const = lambda i: (0, 0)
            qkv = _project(_mla_proj_kernel, xs, mod, layer, g_mix, wd,
                           list(rope_mla) + [_row(mla_q_norm_g[0]), _row(mla_kv_norm_g[0]), wuq, wukv],
                           _rope_specs() + [_row_spec(MLA_Q_LORA), _row_spec(MLA_KV_LORA),
                                            _resident(wuq.shape, const), _resident(wukv.shape, const)],
                           2 * n_q + D, "mla_proj")
            o = _attention(functools.partial(_pair_attn_kernel, n_pairs=1, k_per_head=True,
                                             scale=(MLA_NOPE + MLA_ROPE) ** -0.5), qkv,
                           2 * LANES, 2 * LANES, LANES, LANES, n_q, 2 * n_q, MLA_HEADS // 2,
                           name="mla_attention")
            w_o = mla_w_o[0]
        else:
            qkv = _project(_na_proj_kernel, xs, mod, layer, g_mix, na_w_qkv[0].astype(BF16), [], [], 3 * D, "na_proj")
            o = _na_attention(qkv, _na_bias(na_rpb[0]))
            w_o = na_w_o[0]
        xs = _ffn(xs, mod, layer, norm_g[layer, 2], w_in[layer, 1], w_out[layer, 1], mod_row=6,
                  n_rows=N_LAT if last else N_TOK, attn=(o, w_o.astype(BF16)),
                  final_g=final_g if last else None)
    return xs.reshape(BATCH, SEQ, D)
```

```python
import functools
import math

import jax
import jax.numpy as jnp
from jax import lax
from jax.experimental import pallas as pl
from jax.experimental.pallas import tpu as pltpu

F32 = jnp.float32
BF16 = jnp.bfloat16

D = 1024
BATCH = 8
SEQ = 2048
DEPTH = 4
GRID_W = 64
GRID_ROWS = SEQ // GRID_W
CTX = 256
N_MOD = 9
D_FF = 2816
ROPE_THETA = 10000.0
NEG_INF = -1e30
EPS = 1e-6

N_LAT = BATCH * SEQ
N_TOK = N_LAT + BATCH * CTX
N_GROUPS = N_TOK // SEQ
MOD_ROWS = 16
assert BATCH * CTX == SEQ and N_GROUPS == BATCH + 1 and N_GROUPS <= MOD_ROWS

LANES = 128
MXU_N = 256
HEAD = 64
FF_CHUNKS = D_FF // MXU_N
assert FF_CHUNKS * MXU_N == D_FF

DA_HEADS = 8
GQA_HEADS, GQA_KV = 16, 4
MLA_HEADS, MLA_Q_LORA, MLA_KV_LORA, MLA_NOPE, MLA_ROPE = 16, 256, 128, 64, 32
NA_HEADS, NA_WIN_ROWS, NA_WIN_COLS = 16, 8, 16
NA_TILE_ROWS = 4
NA_TILE_BAND = 12
NA_TILES = GRID_ROWS // NA_TILE_ROWS
NA_VARIANT_TILES = (0, 1, NA_TILES - 1)
NA_TILE_VARIANT = tuple(0 if t == 0 else 2 if t == NA_TILES - 1 else 1 for t in range(NA_TILES))
assert NA_TILE_ROWS + NA_WIN_ROWS <= NA_TILE_BAND and (NA_TILE_BAND * GRID_W) % MXU_N == 0

TM = 512
TQ = 256
MOD_TN = 1024
LAT_TILES = SEQ // TM
QT_PER_BATCH = SEQ // TQ
assert TQ == CTX
VMEM_LIMIT = 56 * 1024 * 1024

NT = (((1,), (1,)), ((), ()))
TN = (((0,), (0,)), ((), ()))


def _params(*sem):
    return pltpu.CompilerParams(dimension_semantics=sem, vmem_limit_bytes=VMEM_LIMIT)


def _resident(shape, index_map):
    return pl.BlockSpec(shape, index_map, pipeline_mode=pl.Buffered(1))


def _bdot(a, b):
    return jnp.dot(a, b, preferred_element_type=F32)


def _rms(x):
    return x * lax.rsqrt(jnp.mean(x * x, axis=-1, keepdims=True) + EPS)


def _modulated(x, g, shift, scale):
    return (_rms(x) * g) * (1.0 + scale) + shift


def _mod_kernel(c_ref, w_ref, b_ref, o_ref):
    c = c_ref[...]
    s = (c * jax.nn.sigmoid(c)).astype(BF16)
    o_ref[...] = _bdot(s, w_ref[...].astype(BF16)) + b_ref[...]


def _modulation(cvec, w_mod, b_mod):
    n = N_MOD * D
    return pl.pallas_call(
        _mod_kernel,
        grid=(DEPTH, n // MOD_TN),
        in_specs=[pl.BlockSpec((MOD_ROWS, D), lambda l, j: (0, 0)),
                  pl.BlockSpec((None, D, MOD_TN), lambda l, j: (l, 0, j)),
                  pl.BlockSpec((None, 1, MOD_TN), lambda l, j: (l, 0, j))],
        out_specs=pl.BlockSpec((None, MOD_ROWS, MOD_TN), lambda l, j: (l, 0, j)),
        out_shape=jax.ShapeDtypeStruct((DEPTH, MOD_ROWS, n), F32),
        compiler_params=_params("arbitrary", "arbitrary"),
        name="modulation",
    )(cvec, w_mod, b_mod.reshape(DEPTH, 1, n))


def _ffn_kernel(*refs, mod_row, has_attn, final_norm):
    it = iter(refs)
    x_ref, mod_ref, g_ref, win_ref, wout_ref = (next(it) for _ in range(5))
    o_ref, wo_ref = (next(it), next(it)) if has_attn else (None, None)
    fg_ref = next(it) if final_norm else None
    out_ref, a_scr = next(it), next(it)

    x = x_ref[...]
    mod = mod_ref[...]
    if has_attn:
        x = x + mod[5:6] * _bdot(o_ref[...], wo_ref[...])
    shift, scale, gate = (mod[mod_row + k:mod_row + k + 1] for k in range(3))
    h = _modulated(x, g_ref[...], shift, scale).astype(BF16)
    for c in range(FF_CHUNKS):
        lo = c * MXU_N
        gt = _bdot(h, win_ref[:, lo:lo + MXU_N])
        up = _bdot(h, win_ref[:, D_FF + lo:D_FF + lo + MXU_N])
        a_scr[:, lo:lo + MXU_N] = ((gt * jax.nn.sigmoid(gt)) * up).astype(BF16)
    y = x + (0.5 * gate) * _bdot(a_scr[...], wout_ref[...])
    if final_norm:
        y = _rms(y) * fg_ref[...]
    out_ref[...] = y


def _ffn(x, mod, layer, g, w_in, w_out, *, mod_row, n_rows, attn=None, final_g=None):
    tok = lambda i: (i, 0)
    const = lambda i: (0, 0)
    in_specs = [pl.BlockSpec((TM, D), tok),
                pl.BlockSpec((None, None, N_MOD, D), lambda i: (layer, i // LAT_TILES, 0, 0)),
                _resident((1, D), const),
                _resident((D, 2 * D_FF), const),
                _resident((D_FF, D), const)]
    args = [x, mod, g.reshape(1, D), w_in, w_out]
    if attn is not None:
        o, w_o = attn
        in_specs += [pl.BlockSpec((TM, D), tok), _resident((D, D), const)]
        args += [o, w_o]
    if final_g is not None:
        in_specs.append(_resident((1, D), const))
        args.append(final_g.reshape(1, D))
    return pl.pallas_call(
        functools.partial(_ffn_kernel, mod_row=mod_row, has_attn=attn is not None,
                          final_norm=final_g is not None),
        grid=(n_rows // TM,),
        in_specs=in_specs,
        out_specs=pl.BlockSpec((TM, D), tok),
        out_shape=jax.ShapeDtypeStruct((n_rows, D), F32),
        scratch_shapes=[pltpu.VMEM((TM, D_FF), BF16)],
        compiler_params=_params("arbitrary"),
        name="ffn",
    )(*args)


def _rope_tables(rd, lane0, period):
    t = jnp.arange(SEQ)
    half, nf = rd // 2, rd // 4
    inv_freq = ROPE_THETA ** (-jnp.arange(nf, dtype=F32) / nf)
    ang_r = (t // GRID_W).astype(F32)[:, None] * inv_freq
    ang_c = (t % GRID_W).astype(F32)[:, None] * inv_freq
    zero = jnp.zeros((SEQ, nf), F32)
    cos = jnp.concatenate([jnp.cos(ang_r)] * 2 + [jnp.cos(ang_c)] * 2, axis=1)
    s_up = jnp.concatenate([-jnp.sin(ang_r), zero, -jnp.sin(ang_c), zero], axis=1)
    s_dn = jnp.concatenate([zero, jnp.sin(ang_r), zero, jnp.sin(ang_c)], axis=1)
    assert half == 2 * nf and LANES % period == 0 and lane0 + rd <= period

    def place(blk, fill):
        pat = jnp.full((SEQ, period), fill, F32).at[:, lane0:lane0 + rd].set(blk)
        pat = jnp.tile(pat, (1, LANES // period))
        return jnp.concatenate([pat, jnp.full((TM, LANES), fill, F32)], axis=0)

    return place(cos, 1.0), place(s_up, 0.0), place(s_dn, 0.0)


def _rope(x, cos, s_up, s_dn, q):
    return x * cos + pltpu.roll(x, LANES - q, 1) * s_up + pltpu.roll(x, q, 1) * s_dn


def _rope_specs():
    m = lambda i: (jnp.where(i < BATCH * LAT_TILES, i % LAT_TILES, LAT_TILES), 0)
    return [pl.BlockSpec((TM, LANES), m)] * 3


def _blocks(n_cols):
    return range(0, n_cols, LANES)


def _head_sumsq(x, ones_blk):
    xx = x * x
    hi = xx.astype(BF16)
    lo = (xx - hi.astype(F32)).astype(BF16)
    return _bdot(hi, ones_blk) + _bdot(lo, ones_blk)


def _mixer_input(x_ref, mod_ref, g_ref):
    mod = mod_ref[...]
    return _modulated(x_ref[...], g_ref[...], mod[3:4], mod[4:5]).astype(BF16)


def _qk_tables(c_ref, su_ref, sd_ref):
    k_tabs = (c_ref[...], su_ref[...], sd_ref[...])
    return tuple(t * HEAD ** -0.5 for t in k_tabs), k_tabs


def _da_proj_kernel(x_ref, mod_ref, g_ref, w_ref, c_ref, su_ref, sd_ref, out_ref):
    h = _mixer_input(x_ref, mod_ref, g_ref)
    q_tabs, k_tabs = _qk_tables(c_ref, su_ref, sd_ref)
    for lo in _blocks(2 * D):
        y = _bdot(h, w_ref[:, lo:lo + LANES])
        out_ref[:, lo:lo + LANES] = _rope(y, *(q_tabs if lo < D else k_tabs), HEAD // 4).astype(BF16)
    out_ref[:, 2 * D:] = _bdot(h, w_ref[:, 2 * D:]).astype(BF16)


def _gqa_proj_kernel(x_ref, mod_ref, g_ref, w_ref, c_ref, su_ref, sd_ref, qg_ref, kg_ref, ones_ref, out_ref):
    h = _mixer_input(x_ref, mod_ref, g_ref)
    q_tabs, k_tabs = _qk_tables(c_ref, su_ref, sd_ref)
    ones_blk = ones_ref[...]
    n_qk = D + GQA_KV * LANES
    for lo in _blocks(n_qk):
        y = _bdot(h, w_ref[:, lo:lo + LANES])
        gain = qg_ref[...] if lo < D else kg_ref[...]
        y = (y * lax.rsqrt(_head_sumsq(y, ones_blk) * (1.0 / HEAD) + EPS)) * gain
        out_ref[:, lo:lo + LANES] = _rope(y, *(q_tabs if lo < D else k_tabs), HEAD // 4).astype(BF16)
    out_ref[:, n_qk:] = _bdot(h, w_ref[:, n_qk:]).astype(BF16)


def _mla_proj_kernel(x_ref, mod_ref, g_ref, wd_ref, c_ref, su_ref, sd_ref, qg_ref, kvg_ref,
                     wuq_ref, wukv_ref, out_ref):
    h = _mixer_input(x_ref, mod_ref, g_ref)
    c, su, sd = c_ref[...], su_ref[...], sd_ref[...]
    d = _bdot(h, wd_ref[...])
    cq = (_rms(d[:, :MLA_Q_LORA]) * qg_ref[...]).astype(BF16)
    ckv = (_rms(d[:, MLA_Q_LORA:MLA_Q_LORA + MLA_KV_LORA]) * kvg_ref[...]).astype(BF16)
    k_pe = _rope(d[:, MLA_Q_LORA + MLA_KV_LORA:], c, su, sd, MLA_ROPE // 4)
    n_q = MLA_HEADS * LANES
    for lo in _blocks(n_q):
        q = _bdot(cq, wuq_ref[:, lo:lo + LANES])
        out_ref[:, lo:lo + LANES] = _rope(q, c, su, sd, MLA_ROPE // 4).astype(BF16)
        k = _bdot(ckv, wukv_ref[:, lo:lo + LANES])
        out_ref[:, n_q + lo:n_q + lo + LANES] = (k + k_pe).astype(BF16)
    out_ref[:, 2 * n_q:] = _bdot(ckv, wukv_ref[:, n_q:]).astype(BF16)


def _na_proj_kernel(x_ref, mod_ref, g_ref, w_ref, out_ref):
    h = _mixer_input(x_ref, mod_ref, g_ref)
    out_ref[:, :D] = (_bdot(h, w_ref[:, :D]) * HEAD ** -0.5).astype(BF16)
    out_ref[:, D:] = _bdot(h, w_ref[:, D:]).astype(BF16)


def _project(kernel, x, mod, layer, g, w, extra, extra_specs, out_cols, name):
    const = lambda i: (0, 0)
    in_specs = [pl.BlockSpec((TM, D), lambda i: (i, 0)),
                pl.BlockSpec((None, None, N_MOD, D), lambda i: (layer, i // LAT_TILES, 0, 0)),
                _resident((1, D), const),
                _resident(w.shape, const)] + extra_specs
    return pl.pallas_call(
        kernel,
        grid=(N_TOK // TM,),
        in_specs=in_specs,
        out_specs=pl.BlockSpec((TM, out_cols), lambda i: (i, 0)),
        out_shape=jax.ShapeDtypeStruct((N_TOK, out_cols), BF16),
        compiler_params=_params("arbitrary"),
        name=name,
    )(x, mod, g.reshape(1, D), w, *extra)


def _row(v):
    return v.reshape(1, -1).astype(F32)


def _row_spec(n):
    return _resident((1, n), lambda i: (0, 0))


def _stack_halves(q_blocks):
    lane = lax.broadcasted_iota(jnp.int32, (1, LANES), 1)
    m0 = (lane < HEAD).astype(F32)
    parts = []
    for q in q_blocks:
        qf = q.astype(F32)
        parts += [(qf * m0).astype(BF16), (qf * (1.0 - m0)).astype(BF16)]
    return jnp.concatenate(parts, axis=0)


def _attend_t(qs, segs, scale=None, bias=None):
    ss = [lax.dot_general(k, qs, NT, preferred_element_type=F32) for k, _ in segs]
    if scale is not None:
        ss = [s * scale for s in ss]
    if bias is not None:
        ss[0] = ss[0] + bias
    m = functools.reduce(jnp.maximum, [jnp.max(s, axis=0, keepdims=True) for s in ss])
    es = [jnp.exp(s - m) for s in ss]
    l = functools.reduce(jnp.add, [jnp.sum(e, axis=0, keepdims=True) for e in es])
    o = functools.reduce(jnp.add, [lax.dot_general(v, e.astype(BF16), TN, preferred_element_type=F32)
                                   for e, (_, v) in zip(es, segs)])
    return o / l


def _pair_out(o_t, p, rows):
    a = o_t[:HEAD, (2 * p) * rows:(2 * p + 1) * rows]
    b = o_t[HEAD:, (2 * p + 1) * rows:(2 * p + 2) * rows]
    return jnp.concatenate([a, b], axis=0).T


def _on_tile_kind(run, kl_ref, kc_ref, vl_ref, vc_ref):
    t = pl.program_id(2)

    @pl.when(t < QT_PER_BATCH)
    def _():
        run([(kl_ref[...], vl_ref[...]), (kc_ref[...], vc_ref[...])])

    @pl.when(t == QT_PER_BATCH)
    def _():
        run([(kc_ref[...], vc_ref[...])])


def _da_attn_kernel(q_ref, kl_ref, kc_ref, vl_ref, vc_ref, lam_ref, g_ref, o_ref, *, lam_init):
    qs = _stack_halves([q_ref[...]])
    lv = lam_ref[...]
    lam = (jnp.exp(jnp.sum(lv[0:1] * lv[1:2], axis=-1, keepdims=True))
           - jnp.exp(jnp.sum(lv[2:3] * lv[3:4], axis=-1, keepdims=True)) + lam_init)

    def run(segs):
        o_t = _attend_t(qs, segs)
        o = o_t[:, :TQ] - lam * o_t[:, TQ:]
        o = o * lax.rsqrt(jnp.mean(o * o, axis=0, keepdims=True) + EPS)
        o_ref[...] = ((o.T * g_ref[...]) * (1.0 - lam_init)).astype(BF16)

    _on_tile_kind(run, kl_ref, kc_ref, vl_ref, vc_ref)


def _pair_attn_kernel(q_ref, kl_ref, kc_ref, vl_ref, vc_ref, o_ref, *, wide_keys, scale):
    if wide_keys:
        qa, qb = q_ref[:, :LANES], q_ref[:, LANES:]
        zero = jnp.zeros_like(qa)
        qs = jnp.concatenate([jnp.concatenate([qa, zero], axis=1),
                              jnp.concatenate([zero, qb], axis=1)], axis=0)
        n_pairs = 1
    else:
        n_pairs = q_ref.shape[1] // LANES
        qs = _stack_halves([q_ref[:, p * LANES:(p + 1) * LANES] for p in range(n_pairs)])

    def run(segs):
        o_t = _attend_t(qs, segs, scale)
        for p in range(n_pairs):
            o_ref[:, p * LANES:(p + 1) * LANES] = _pair_out(o_t, p, TQ).astype(BF16)

    _on_tile_kind(run, kl_ref, kc_ref, vl_ref, vc_ref)


def _attention(kernel, qkv, q_cols, k_cols, v_cols, o_cols, k0, v0, n_hb, extra=(), extra_specs=(), name=""):
    qrow = lambda b, t: jnp.where(t < QT_PER_BATCH, b * QT_PER_BATCH + t, N_LAT // TQ + b)
    kb, vb = k0 // k_cols, v0 // v_cols
    in_specs = [pl.BlockSpec((TQ, q_cols), lambda b, h, t: (qrow(b, t), h)),
                pl.BlockSpec((SEQ, k_cols), lambda b, h, t: (b, kb + h)),
                pl.BlockSpec((CTX, k_cols), lambda b, h, t: (N_LAT // CTX + b, kb + h)),
                pl.BlockSpec((SEQ, v_cols), lambda b, h, t: (b, vb + h)),
                pl.BlockSpec((CTX, v_cols), lambda b, h, t: (N_LAT // CTX + b, vb + h))] + list(extra_specs)
    return pl.pallas_call(
        kernel,
        grid=(BATCH, n_hb, QT_PER_BATCH + 1),
        in_specs=in_specs,
        out_specs=pl.BlockSpec((TQ, o_cols), lambda b, h, t: (qrow(b, t), h)),
        out_shape=jax.ShapeDtypeStruct((N_TOK, D), BF16),
        compiler_params=_params("arbitrary", "arbitrary", "arbitrary"),
        name=name,
    )(qkv, qkv, qkv, qkv, qkv, *extra)


def _na_tile_band(t):
    r0 = t * NA_TILE_ROWS
    return r0, min(max(r0 - NA_WIN_ROWS // 2, 0), GRID_ROWS - NA_TILE_BAND)


def _na_attn_kernel(q_ref, kl_ref, kc_ref, vl_ref, vc_ref, bias_ref, o_ref):
    ctx_seg = (kc_ref[...], vc_ref[...])
    for t in range(NA_TILES):
        r0, b0 = _na_tile_band(t)
        rows = slice(r0 * GRID_W, (r0 + NA_TILE_ROWS) * GRID_W)
        band = slice(b0 * GRID_W, (b0 + NA_TILE_BAND) * GRID_W)
        qs = _stack_halves([q_ref[rows, :]])
        o_t = _attend_t(qs, [(kl_ref[band, :], vl_ref[band, :]), ctx_seg], bias=bias_ref[NA_TILE_VARIANT[t]])
        o_ref[rows, :] = _pair_out(o_t, 0, NA_TILE_ROWS * GRID_W).astype(BF16)


def _na_bias(rpb):
    cols = jnp.arange(GRID_W)
    col_start = jnp.clip(cols - NA_WIN_COLS // 2, 0, GRID_W - NA_WIN_COLS)
    col_in = (cols[None, :] >= col_start[:, None]) & (cols[None, :] < col_start[:, None] + NA_WIN_COLS)
    col_idx = jnp.clip(cols[None, :] - cols[:, None] + NA_WIN_COLS - 1, 0, 2 * NA_WIN_COLS - 2)
    n_k, n_q = NA_TILE_BAND * GRID_W, NA_TILE_ROWS * GRID_W
    tabs = []
    for t in NA_VARIANT_TILES:
        r0, b0 = _na_tile_band(t)
        r = r0 + jnp.arange(NA_TILE_ROWS)
        rs = jnp.clip(r - NA_WIN_ROWS // 2, 0, GRID_ROWS - NA_WIN_ROWS)
        ka = b0 + jnp.arange(NA_TILE_BAND)
        row_in = (ka[None, :] >= rs[:, None]) & (ka[None, :] < rs[:, None] + NA_WIN_ROWS)
        row_idx = jnp.clip(ka[None, :] - r[:, None] + NA_WIN_ROWS - 1, 0, 2 * NA_WIN_ROWS - 2)
        b = rpb[:, row_idx[:, :, None, None], col_idx[None, None, :, :]]
        b = jnp.where((row_in[:, :, None, None] & col_in[None, None, :, :])[None], b, NEG_INF)
        b = jnp.transpose(b, (0, 2, 4, 1, 3)).reshape(NA_HEADS // 2, 2, n_k, n_q)
        tabs.append(jnp.transpose(b, (0, 2, 1, 3)).reshape(NA_HEADS // 2, n_k, 2 * n_q))
    return jnp.stack(tabs).astype(F32)


def _na_attention(qkv, bias):
    n_hb = NA_HEADS // 2
    kb, vb = D // LANES, 2 * D // LANES
    in_specs = [pl.BlockSpec((SEQ, LANES), lambda h, b: (b, h)),
                pl.BlockSpec((SEQ, LANES), lambda h, b: (b, kb + h)),
                pl.BlockSpec((CTX, LANES), lambda h, b: (N_LAT // CTX + b, kb + h)),
                pl.BlockSpec((SEQ, LANES), lambda h, b: (b, vb + h)),
                pl.BlockSpec((CTX, LANES), lambda h, b: (N_LAT // CTX + b, vb + h)),
                pl.BlockSpec((len(NA_VARIANT_TILES), None) + bias.shape[2:], lambda h, b: (0, h, 0, 0))]
    return pl.pallas_call(
        _na_attn_kernel,
        grid=(n_hb, BATCH),
        in_specs=in_specs,
        out_specs=pl.BlockSpec((SEQ, LANES), lambda h, b: (b, h)),
        out_shape=jax.ShapeDtypeStruct((N_LAT, D), BF16),
        compiler_params=_params("arbitrary", "arbitrary"),
        name="na_attention",
    )(qkv, qkv, qkv, qkv, qkv, bias)


def _dup_heads(w, n_heads):
    w = w.reshape(D, n_heads, 1, HEAD)
    return jnp.broadcast_to(w, (D, n_heads, LANES // HEAD, HEAD)).reshape(D, n_heads * LANES)


def _pad_heads(w, width):
    k = w.shape[0]
    w = w.reshape(k, MLA_HEADS, width)
    return jnp.pad(w, ((0, 0), (0, 0), (0, LANES - width))).reshape(k, MLA_HEADS * LANES)


def kernel(x, c, ctx, c_ctx, w_mod, b_mod, norm_g, w_ffn_in, w_ffn_out, da_w_qkv, da_lam_q1, da_lam_k1, da_lam_q2, da_lam_k2, da_subln_g, da_w_o, gqa_w_qkv, gqa_q_norm_g, gqa_k_norm_g, gqa_w_o, mla_w_down, mla_q_norm_g, mla_kv_norm_g, mla_w_uq, mla_w_ukv, mla_w_o, na_w_qkv, na_rpb, na_w_o, final_g):
    assert x.shape == (BATCH, SEQ, D) and ctx.shape == (BATCH, CTX, D) and w_mod.shape[0] == DEPTH == 4

    cvec = jnp.zeros((MOD_ROWS, D), F32).at[:BATCH].set(c).at[BATCH].set(c_ctx)
    mod = _modulation(cvec, w_mod, b_mod).reshape(DEPTH, MOD_ROWS, N_MOD, D)
    xs = jnp.concatenate([x.reshape(N_LAT, D), ctx.reshape(BATCH * CTX, D)], axis=0)
    w_in, w_out = w_ffn_in.astype(BF16), w_ffn_out.astype(BF16)

    rope64 = _rope_tables(HEAD, 0, HEAD)
    rope_mla = _rope_tables(MLA_ROPE, MLA_NOPE, LANES)
    ones_blk = jnp.kron(jnp.eye(LANES // HEAD, dtype=F32), jnp.ones((HEAD, HEAD), F32)).astype(BF16)
    blk_spec = _resident((LANES, LANES), lambda i: (0, 0))

    for layer in range(DEPTH):
        last = layer == DEPTH - 1
        xs = _ffn(xs, mod, layer, norm_g[layer, 0], w_in[layer, 0], w_out[layer, 0], mod_row=0, n_rows=N_TOK)
        g_mix = norm_g[layer, 1]
        if layer == 0:
            qkv = _project(_da_proj_kernel, xs, mod, layer, g_mix, da_w_qkv[0].astype(BF16),
                           list(rope64), _rope_specs(), 3 * D, "da_proj")
            lam_init = 0.8 - 0.6 * math.exp(-0.3 * layer)
            lam = jnp.zeros((8, LANES), F32).at[:4, :HEAD].set(
                jnp.stack([da_lam_q1[0], da_lam_k1[0], da_lam_q2[0], da_lam_k2[0]]))
            const3 = lambda b, h, t: (0, 0)
            o = _attention(functools.partial(_da_attn_kernel, lam_init=lam_init), qkv,
                           LANES, LANES, LANES, LANES, D, 2 * D, DA_HEADS,
                           extra=(lam, _row(da_subln_g[0])),
                           extra_specs=(pl.BlockSpec((8, LANES), const3), pl.BlockSpec((1, LANES), const3)),
                           name="da_attention")
            w_o = da_w_o[0]
        elif layer == 1:
            wq, wk, wv = jnp.split(gqa_w_qkv[0], [GQA_HEADS * HEAD, (GQA_HEADS + GQA_KV) * HEAD], axis=1)
            w = jnp.concatenate([wq, _dup_heads(wk, GQA_KV), _dup_heads(wv, GQA_KV)], axis=1).astype(BF16)
            gains = (_row(jnp.tile(gqa_q_norm_g[0], LANES // HEAD)), _row(jnp.tile(gqa_k_norm_g[0], LANES // HEAD)))
            qkv = _project(_gqa_proj_kernel, xs, mod, layer, g_mix, w,
                           list(rope64) + list(gains) + [ones_blk],
                           _rope_specs() + [_row_spec(LANES)] * 2 + [blk_spec],
                           D + 2 * GQA_KV * LANES, "gqa_proj")
            o = _attention(functools.partial(_pair_attn_kernel, wide_keys=False, scale=None), qkv,
                           2 * LANES, LANES, LANES, 2 * LANES, D, D + GQA_KV * LANES, GQA_KV,
                           name="gqa_attention")
            w_o = gqa_w_o[0]
        elif layer == 2:
            wd = mla_w_down[0]
            n_lora = MLA_Q_LORA + MLA_KV_LORA
            wd = jnp.concatenate([wd[:, :n_lora], jnp.zeros((D, MLA_NOPE), F32), wd[:, n_lora:],
                                  jnp.zeros((D, LANES - MLA_NOPE - MLA_ROPE), F32)], axis=1).astype(BF16)
            wuq = _pad_heads(mla_w_uq[0], MLA_NOPE + MLA_ROPE).astype(BF16)
            wukv = mla_w_ukv[0].reshape(MLA_KV_LORA, MLA_HEADS, 2 * HEAD)
            wuk = _pad_heads(wukv[:, :, :MLA_NOPE].reshape(MLA_KV_LORA, -1), MLA_NOPE)
            wuv = wukv[:, :, MLA_NOPE:].reshape(MLA_KV_LORA, MLA_HEADS * HEAD)
            wukv = jnp.concatenate([wuk, wuv], axis=1).astype(BF16)
            n_q = MLA_HEADS * LANES
            const = lambda i: (0, 0)
            qkv = _project(_mla_proj_kernel, xs, mod, layer, g_mix, wd,
                           list(rope_mla) + [_row(mla_q_norm_g[0]), _row(mla_kv_norm_g[0]), wuq, wukv],
                           _rope_specs() + [_row_spec(MLA_Q_LORA), _row_spec(MLA_KV_LORA),
                                            _resident(wuq.shape, const), _resident(wukv.shape, const)],
                           2 * n_q + D, "mla_proj")
            o = _attention(functools.partial(_pair_attn_kernel, wide_keys=True,
                                             scale=(MLA_NOPE + MLA_ROPE) ** -0.5), qkv,
                           2 * LANES, 2 * LANES, LANES, LANES, n_q, 2 * n_q, MLA_HEADS // 2,
                           name="mla_attention")
            w_o = mla_w_o[0]
        else:
            qkv = _project(_na_proj_kernel, xs, mod, layer, g_mix, na_w_qkv[0].astype(BF16), [], [], 3 * D, "na_proj")
            o = _na_attention(qkv, _na_bias(na_rpb[0]))
            w_o = na_w_o[0]
        xs = _ffn(xs, mod, layer, norm_g[layer, 2], w_in[layer, 1], w_out[layer, 1], mod_row=6,
                  n_rows=N_LAT if last else N_TOK, attn=(o, w_o.astype(BF16)),
                  final_g=final_g if last else None)
    return xs.reshape(BATCH, SEQ, D)
```

```python
import collections
import functools
import math

import jax
import jax.numpy as jnp
import numpy as np
from jax import lax
from jax.experimental import pallas as pl
from jax.experimental.pallas import tpu as pltpu

F32 = jnp.float32
BF16 = jnp.bfloat16

D = 1024
BATCH = 8
SEQ = 2048
DEPTH = 4
GRID_W = 64
GRID_ROWS = SEQ // GRID_W
CTX = 256
N_MOD = 9
D_FF = 2816
ROPE_THETA = 10000.0
NEG_INF = -1e30
EPS = 1e-6
LOG2E = math.log2(math.e)

N_LAT = BATCH * SEQ
N_TOK = N_LAT + BATCH * CTX
N_GROUPS = N_TOK // SEQ
MOD_ROWS = 16
assert BATCH * CTX == SEQ and N_GROUPS == BATCH + 1 and N_GROUPS <= MOD_ROWS

LANES = 128
MXU_N = 256
HEAD = 64
FF_CHUNKS = D_FF // MXU_N
assert FF_CHUNKS * MXU_N == D_FF

DA_HEADS = 8
GQA_HEADS, GQA_KV = 16, 4
MLA_HEADS, MLA_Q_LORA, MLA_KV_LORA, MLA_NOPE, MLA_ROPE = 16, 256, 128, 64, 32
NA_HEADS, NA_WIN_ROWS, NA_WIN_COLS = 16, 8, 16
NA_TILE_ROWS = 4
NA_TILE_BAND = 12
NA_TILES = GRID_ROWS // NA_TILE_ROWS
NA_VARIANT_TILES = (0, 1, NA_TILES - 1)
NA_TILE_VARIANT = tuple(0 if t == 0 else 2 if t == NA_TILES - 1 else 1 for t in range(NA_TILES))
assert NA_TILE_ROWS + NA_WIN_ROWS <= NA_TILE_BAND and (NA_TILE_BAND * GRID_W) % MXU_N == 0

TM = 512
TQ = 1024
SUB_Q = 256
CHAIN_N = 2 * SUB_Q
ATT_KC = 1024
MOD_TN = 1024
LAT_TILES = SEQ // TM
QT_PER_BATCH = SEQ // TQ
assert SUB_Q == CTX == NA_TILE_ROWS * GRID_W and TQ % SUB_Q == 0
VMEM_LIMIT = 56 * 1024 * 1024

NT = (((1,), (1,)), ((), ()))
TN = (((0,), (0,)), ((), ()))


def _params(*sem):
    return pltpu.CompilerParams(dimension_semantics=sem, vmem_limit_bytes=VMEM_LIMIT)


def _resident(shape, index_map):
    return pl.BlockSpec(shape, index_map, pipeline_mode=pl.Buffered(1))


def _bdot(a, b):
    return jnp.dot(a, b, preferred_element_type=F32)


def _rms(x):
    return x * lax.rsqrt(jnp.mean(x * x, axis=-1, keepdims=True) + EPS)


def _modulated(x, g, shift, scale):
    return (_rms(x) * g) * (1.0 + scale) + shift


def _mod_kernel(c_ref, w_ref, b_ref, o_ref):
    c = c_ref[...]
    s = (c * jax.nn.sigmoid(c)).astype(BF16)
    o_ref[...] = _bdot(s, w_ref[...].astype(BF16)) + b_ref[...]


def _modulation(cvec, w_mod, b_mod):
    n = N_MOD * D
    return pl.pallas_call(
        _mod_kernel,
        grid=(DEPTH, n // MOD_TN),
        in_specs=[pl.BlockSpec((MOD_ROWS, D), lambda l, j: (0, 0)),
                  pl.BlockSpec((None, D, MOD_TN), lambda l, j: (l, 0, j)),
                  pl.BlockSpec((None, 1, MOD_TN), lambda l, j: (l, 0, j))],
        out_specs=pl.BlockSpec((None, MOD_ROWS, MOD_TN), lambda l, j: (l, 0, j)),
        out_shape=jax.ShapeDtypeStruct((DEPTH, MOD_ROWS, n), F32),
        compiler_params=_params("arbitrary", "arbitrary"),
        name="modulation",
    )(cvec, w_mod, b_mod.reshape(DEPTH, 1, n))


def _ffn_kernel(*refs, mod_row, has_attn, final_norm):
    it = iter(refs)
    x_ref, mod_ref, g_ref, win_ref, wout_ref = (next(it) for _ in range(5))
    o_ref, wo_ref = (next(it), next(it)) if has_attn else (None, None)
    fg_ref = next(it) if final_norm else None
    out_ref, a_scr = next(it), next(it)

    x = x_ref[...]
    mod = mod_ref[...]
    if has_attn:
        x = x + mod[5:6] * _bdot(o_ref[...], wo_ref[...])
    shift, scale, gate = (mod[mod_row + k:mod_row + k + 1] for k in range(3))
    h = _modulated(x, g_ref[...], shift, scale).astype(BF16)
    for c in range(FF_CHUNKS):
        lo = c * MXU_N
        gt = _bdot(h, win_ref[:, lo:lo + MXU_N])
        up = _bdot(h, win_ref[:, D_FF + lo:D_FF + lo + MXU_N])
        a_scr[:, lo:lo + MXU_N] = ((gt * jax.nn.sigmoid(gt)) * up).astype(BF16)
    y = x + (0.5 * gate) * _bdot(a_scr[...], wout_ref[...])
    if final_norm:
        y = _rms(y) * fg_ref[...]
    out_ref[...] = y


def _ffn(x, mod, layer, g, w_in, w_out, *, mod_row, n_rows, attn=None, final_g=None):
    tok = lambda i: (i, 0)
    const = lambda i: (0, 0)
    in_specs = [pl.BlockSpec((TM, D), tok),
                pl.BlockSpec((None, None, N_MOD, D), lambda i: (layer, i // LAT_TILES, 0, 0)),
                _resident((1, D), const),
                _resident((D, 2 * D_FF), const),
                _resident((D_FF, D), const)]
    args = [x, mod, g.reshape(1, D), w_in, w_out]
    if attn is not None:
        o, w_o = attn
        in_specs += [pl.BlockSpec((TM, D), tok), _resident((D, D), const)]
        args += [o, w_o]
    if final_g is not None:
        in_specs.append(_resident((1, D), const))
        args.append(final_g.reshape(1, D))
    return pl.pallas_call(
        functools.partial(_ffn_kernel, mod_row=mod_row, has_attn=attn is not None,
                          final_norm=final_g is not None),
        grid=(n_rows // TM,),
        in_specs=in_specs,
        out_specs=pl.BlockSpec((TM, D), tok),
        out_shape=jax.ShapeDtypeStruct((n_rows, D), F32),
        scratch_shapes=[pltpu.VMEM((TM, D_FF), BF16)],
        compiler_params=_params("arbitrary"),
        name="ffn",
    )(*args)


def _rope_tables(rd, lane0, period):
    t = jnp.arange(SEQ)
    half, nf = rd // 2, rd // 4
    inv_freq = ROPE_THETA ** (-jnp.arange(nf, dtype=F32) / nf)
    ang_r = (t // GRID_W).astype(F32)[:, None] * inv_freq
    ang_c = (t % GRID_W).astype(F32)[:, None] * inv_freq
    zero = jnp.zeros((SEQ, nf), F32)
    cos = jnp.concatenate([jnp.cos(ang_r)] * 2 + [jnp.cos(ang_c)] * 2, axis=1)
    s_up = jnp.concatenate([-jnp.sin(ang_r), zero, -jnp.sin(ang_c), zero], axis=1)
    s_dn = jnp.concatenate([zero, jnp.sin(ang_r), zero, jnp.sin(ang_c)], axis=1)
    assert half == 2 * nf and LANES % period == 0 and lane0 + rd <= period

    def place(blk, fill):
        pat = jnp.full((SEQ, period), fill, F32).at[:, lane0:lane0 + rd].set(blk)
        pat = jnp.tile(pat, (1, LANES // period))
        return jnp.concatenate([pat, jnp.full((TM, LANES), fill, F32)], axis=0)

    return place(cos, 1.0), place(s_up, 0.0), place(s_dn, 0.0)


def _rope(x, cos, s_up, s_dn, q):
    return x * cos + pltpu.roll(x, LANES - q, 1) * s_up + pltpu.roll(x, q, 1) * s_dn


def _rope_specs():
    m = lambda i: (jnp.where(i < BATCH * LAT_TILES, i % LAT_TILES, LAT_TILES), 0)
    return [pl.BlockSpec((TM, LANES), m)] * 3


def _blocks(n_cols):
    return range(0, n_cols, LANES)


def _head_sumsq(x, ones_blk):
    xx = x * x
    hi = xx.astype(BF16)
    lo = (xx - hi.astype(F32)).astype(BF16)
    return _bdot(hi, ones_blk) + _bdot(lo, ones_blk)


def _mixer_input(x_ref, mod_ref, g_ref):
    mod = mod_ref[...]
    return _modulated(x_ref[...], g_ref[...], mod[3:4], mod[4:5]).astype(BF16)


def _qk_tables(c_ref, su_ref, sd_ref):
    k_tabs = (c_ref[...], su_ref[...], sd_ref[...])
    return tuple(t * (HEAD ** -0.5 * LOG2E) for t in k_tabs), k_tabs


def _da_proj_kernel(x_ref, mod_ref, g_ref, w_ref, c_ref, su_ref, sd_ref, out_ref):
    h = _mixer_input(x_ref, mod_ref, g_ref)
    q_tabs, k_tabs = _qk_tables(c_ref, su_ref, sd_ref)
    for lo in _blocks(2 * D):
        y = _bdot(h, w_ref[:, lo:lo + LANES])
        out_ref[:, lo:lo + LANES] = _rope(y, *(q_tabs if lo < D else k_tabs), HEAD // 4).astype(BF16)
    out_ref[:, 2 * D:] = _bdot(h, w_ref[:, 2 * D:]).astype(BF16)


def _gqa_proj_kernel(x_ref, mod_ref, g_ref, w_ref, c_ref, su_ref, sd_ref, qg_ref, kg_ref, ones_ref, out_ref):
    h = _mixer_input(x_ref, mod_ref, g_ref)
    q_tabs, k_tabs = _qk_tables(c_ref, su_ref, sd_ref)
    ones_blk = ones_ref[...]
    n_qk = D + GQA_KV * LANES
    for lo in _blocks(n_qk):
        y = _bdot(h, w_ref[:, lo:lo + LANES])
        gain = qg_ref[...] if lo < D else kg_ref[...]
        y = (y * lax.rsqrt(_head_sumsq(y, ones_blk) * (1.0 / HEAD) + EPS)) * gain
        out_ref[:, lo:lo + LANES] = _rope(y, *(q_tabs if lo < D else k_tabs), HEAD // 4).astype(BF16)
    out_ref[:, n_qk:] = _bdot(h, w_ref[:, n_qk:]).astype(BF16)


def _mla_proj_kernel(x_ref, mod_ref, g_ref, wd_ref, c_ref, su_ref, sd_ref, qg_ref, kvg_ref,
                     wuq_ref, wukv_ref, out_ref):
    h = _mixer_input(x_ref, mod_ref, g_ref)
    c, su, sd = c_ref[...], su_ref[...], sd_ref[...]
    d = _bdot(h, wd_ref[...])
    cq = (_rms(d[:, :MLA_Q_LORA]) * qg_ref[...]).astype(BF16)
    ckv = (_rms(d[:, MLA_Q_LORA:MLA_Q_LORA + MLA_KV_LORA]) * kvg_ref[...]).astype(BF16)
    k_pe = _rope(d[:, MLA_Q_LORA + MLA_KV_LORA:], c, su, sd, MLA_ROPE // 4)
    n_q = MLA_HEADS * LANES
    for lo in _blocks(n_q):
        q = _bdot(cq, wuq_ref[:, lo:lo + LANES])
        q = _rope(q, c, su, sd, MLA_ROPE // 4) * ((MLA_NOPE + MLA_ROPE) ** -0.5 * LOG2E)
        out_ref[:, lo:lo + LANES] = q.astype(BF16)
        k = _bdot(ckv, wukv_ref[:, lo:lo + LANES])
        out_ref[:, n_q + lo:n_q + lo + LANES] = (k + k_pe).astype(BF16)
    out_ref[:, 2 * n_q:] = _bdot(ckv, wukv_ref[:, n_q:]).astype(BF16)


def _na_proj_kernel(x_ref, mod_ref, g_ref, w_ref, out_ref):
    h = _mixer_input(x_ref, mod_ref, g_ref)
    out_ref[:, :D] = (_bdot(h, w_ref[:, :D]) * (HEAD ** -0.5 * LOG2E)).astype(BF16)
    out_ref[:, D:] = _bdot(h, w_ref[:, D:]).astype(BF16)


def _project(kernel, x, mod, layer, g, w, extra, extra_specs, out_cols, name):
    const = lambda i: (0, 0)
    in_specs = [pl.BlockSpec((TM, D), lambda i: (i, 0)),
                pl.BlockSpec((None, None, N_MOD, D), lambda i: (layer, i // LAT_TILES, 0, 0)),
                _resident((1, D), const),
                _resident(w.shape, const)] + extra_specs
    return pl.pallas_call(
        kernel,
        grid=(N_TOK // TM,),
        in_specs=in_specs,
        out_specs=pl.BlockSpec((TM, out_cols), lambda i: (i, 0)),
        out_shape=jax.ShapeDtypeStruct((N_TOK, out_cols), BF16),
        compiler_params=_params("arbitrary"),
        name=name,
    )(x, mod, g.reshape(1, D), w, *extra)


def _row(v):
    return v.reshape(1, -1).astype(F32)


def _row_spec(n):
    return _resident((1, n), lambda i: (0, 0))


def _stack_halves(q):
    lane = lax.broadcasted_iota(jnp.int32, (1, LANES), 1)
    m0 = (lane < HEAD).astype(F32)
    qf = q.astype(F32)
    return jnp.concatenate([(qf * m0).astype(BF16), (qf * (1.0 - m0)).astype(BF16)], axis=0)


def _stack_wide(q):
    qa, qb = q[:, :LANES], q[:, LANES:]
    zero = jnp.zeros_like(qa)
    return jnp.concatenate([jnp.concatenate([qa, zero], axis=1), jnp.concatenate([zero, qb], axis=1)], axis=0)


def _pair_out(o_t):
    rows = o_t.shape[1] // 2
    return jnp.concatenate([o_t[:HEAD, :rows], o_t[HEAD:, rows:]], axis=0).T


_Chain = collections.namedtuple("_Chain", "qs segs bias emit")


def _run_chains(chains, s_scr):
    chunks = [[(k, v, lo + c, min(ATT_KC, n - c)) for k, v, lo, n in ch.segs for c in range(0, n, ATT_KC)]
              for ch in chains]
    offs = [[sum(n for _, _, _, n in cl[:j]) for j in range(len(cl))] for cl in chunks]
    stats = [dict(m=None, l=None, acc=None) for _ in chains]

    def score(i, j):
        ch, st = chains[i], stats[i]
        k_ref, _, lo, n = chunks[i][j]
        s = lax.dot_general(k_ref[lo:lo + n, :], ch.qs, NT, preferred_element_type=F32)
        b = ch.bias(offs[i][j], n) if ch.bias is not None else None
        if b is not None:
            s = s + b
        s_scr[i % 2, offs[i][j]:offs[i][j] + n, :] = s
        mj = jnp.max(s, axis=0, keepdims=True)
        st["m"] = mj if st["m"] is None else jnp.maximum(st["m"], mj)

    def value(i, j):
        st = stats[i]
        _, v_ref, lo, n = chunks[i][j]
        e = jnp.exp2(s_scr[i % 2, offs[i][j]:offs[i][j] + n, :] - st["m"])
        lj = jnp.sum(e, axis=0, keepdims=True)
        pv = lax.dot_general(v_ref[lo:lo + n, :], e.astype(BF16), TN, preferred_element_type=F32)
        st["l"] = lj if st["l"] is None else st["l"] + lj
        st["acc"] = pv if st["acc"] is None else st["acc"] + pv

    for i in range(len(chains) + 1):
        n_score = len(chunks[i]) if i < len(chains) else 0
        n_value = len(chunks[i - 1]) if i > 0 else 0
        for j in range(max(n_score, n_value)):
            if j < n_score:
                score(i, j)
            if j < n_value:
                value(i - 1, j)
        if i > 0:
            chains[i - 1].emit(stats[i - 1]["acc"] / stats[i - 1]["l"])


def _attn_refs(refs, n_seg, n_extra):
    kv = refs[1:1 + 2 * n_seg]
    segs = [(kv[2 * s], kv[2 * s + 1], 0, kv[2 * s].shape[0]) for s in range(n_seg)]
    return refs[0], segs, refs[1 + 2 * n_seg:1 + 2 * n_seg + n_extra], refs[-2], refs[-1]


def _da_attn_kernel(*refs, n_seg, lam_init):
    q_ref, segs, (lam_ref, g_ref), o_ref, s_scr = _attn_refs(refs, n_seg, 2)
    lv = lam_ref[...]
    lam = (jnp.exp(jnp.sum(lv[0:1] * lv[1:2], axis=-1, keepdims=True))
           - jnp.exp(jnp.sum(lv[2:3] * lv[3:4], axis=-1, keepdims=True)) + lam_init)

    def emit(rows):
        def f(o_t):
            o = o_t[:, :SUB_Q] - lam * o_t[:, SUB_Q:]
            o = o * lax.rsqrt(jnp.mean(o * o, axis=0, keepdims=True) + EPS)
            o_ref[rows, :] = ((o.T * g_ref[...]) * (1.0 - lam_init)).astype(BF16)
        return f

    chains = []
    for lo in range(0, q_ref.shape[0], SUB_Q):
        rows = slice(lo, lo + SUB_Q)
        chains.append(_Chain(_stack_halves(q_ref[rows, :]), segs, None, emit(rows)))
    _run_chains(chains, s_scr)


def _pair_attn_kernel(*refs, n_seg, wide_keys):
    q_ref, segs, _, o_ref, s_scr = _attn_refs(refs, n_seg, 0)

    def emit(rows, cols):
        def f(o_t):
            o_ref[rows, cols] = _pair_out(o_t).astype(BF16)
        return f

    chains = []
    for lo in range(0, q_ref.shape[0], SUB_Q):
        rows = slice(lo, lo + SUB_Q)
        if wide_keys:
            chains.append(_Chain(_stack_wide(q_ref[rows, :]), segs, None, emit(rows, slice(0, LANES))))
        else:
            for c in _blocks(q_ref.shape[1]):
                cols = slice(c, c + LANES)
                chains.append(_Chain(_stack_halves(q_ref[rows, cols]), segs, None, emit(rows, cols)))
    _run_chains(chains, s_scr)


def _attention(kernel, qkv, q_cols, k_cols, v_cols, o_cols, k0, v0, n_hb, extra=(), extra_specs=(), name=""):
    kb, vb = k0 // k_cols, v0 // v_cols
    ctx_blk = N_LAT // CTX
    scratch = lambda n_keys: [pltpu.VMEM((2, n_keys, CHAIN_N), F32)]
    shape = jax.ShapeDtypeStruct((N_TOK, D), BF16)
    lat = pl.pallas_call(
        functools.partial(kernel, n_seg=2),
        grid=(BATCH, n_hb, QT_PER_BATCH),
        in_specs=[pl.BlockSpec((TQ, q_cols), lambda b, h, t: (b * QT_PER_BATCH + t, h)),
                  pl.BlockSpec((SEQ, k_cols), lambda b, h, t: (b, kb + h)),
                  pl.BlockSpec((SEQ, v_cols), lambda b, h, t: (b, vb + h)),
                  pl.BlockSpec((CTX, k_cols), lambda b, h, t: (ctx_blk + b, kb + h)),
                  pl.BlockSpec((CTX, v_cols), lambda b, h, t: (ctx_blk + b, vb + h))] + list(extra_specs),
        out_specs=pl.BlockSpec((TQ, o_cols), lambda b, h, t: (b * QT_PER_BATCH + t, h)),
        out_shape=shape,
        scratch_shapes=scratch(SEQ + CTX),
        compiler_params=_params("arbitrary", "arbitrary", "arbitrary"),
        name=name,
    )(qkv, qkv, qkv, qkv, qkv, *extra)
    return pl.pallas_call(
        functools.partial(kernel, n_seg=1),
        grid=(BATCH, n_hb),
        in_specs=[pl.BlockSpec((CTX, q_cols), lambda b, h: (ctx_blk + b, h)),
                  pl.BlockSpec((CTX, k_cols), lambda b, h: (ctx_blk + b, kb + h)),
                  pl.BlockSpec((CTX, v_cols), lambda b, h: (ctx_blk + b, vb + h))] + list(extra_specs)
                 + [pl.BlockSpec(memory_space=pl.ANY)],
        out_specs=pl.BlockSpec((CTX, o_cols), lambda b, h: (ctx_blk + b, h)),
        out_shape=shape,
        input_output_aliases={3 + len(extra): 0},
        scratch_shapes=scratch(CTX),
        compiler_params=_params("arbitrary", "arbitrary"),
        name=name + "_ctx",
    )(qkv, qkv, qkv, *extra, lat)


def _na_tile_band(t):
    r0 = t * NA_TILE_ROWS
    return r0, min(max(r0 - NA_WIN_ROWS // 2, 0), GRID_ROWS - NA_TILE_BAND)


def _na_attn_kernel(q_ref, kl_ref, vl_ref, kc_ref, vc_ref, bias_ref, o_ref, s_scr):
    n_band = NA_TILE_BAND * GRID_W

    def emit(rows):
        def f(o_t):
            o_ref[rows, :] = _pair_out(o_t).astype(BF16)
        return f

    def band_bias(variant):
        return lambda off, n: bias_ref[variant, off:off + n, :] if off + n <= n_band else None

    chains = []
    for t in range(NA_TILES):
        r0, b0 = _na_tile_band(t)
        rows = slice(r0 * GRID_W, (r0 + NA_TILE_ROWS) * GRID_W)
        segs = [(kl_ref, vl_ref, b0 * GRID_W, n_band), (kc_ref, vc_ref, 0, CTX)]
        chains.append(_Chain(_stack_halves(q_ref[rows, :]), segs, band_bias(NA_TILE_VARIANT[t]), emit(rows)))
    _run_chains(chains, s_scr)


def _toeplitz(v):
    lead, w, period = v.shape[:-1], NA_WIN_COLS, 2 * GRID_W - 1
    line = jnp.zeros(lead + (period,), v.dtype)
    line = line.at[..., :w].set(v[..., w - 1:]).at[..., period - (w - 1):].set(v[..., :w - 1])
    x = jnp.broadcast_to(line[..., None, :], lead + (GRID_W, period)).reshape(lead + (GRID_W * period,))
    skew = x[..., :GRID_W * (period - 1)].reshape(lead + (GRID_W, period - 1))
    return jnp.swapaxes(skew[..., :GRID_W], -1, -2)


def _na_bias(rpb):
    cols = np.arange(GRID_W)
    col_start = np.clip(cols - NA_WIN_COLS // 2, 0, GRID_W - NA_WIN_COLS)
    col_in = (cols[:, None] >= col_start[None, :]) & (cols[:, None] < col_start[None, :] + NA_WIN_COLS)
    blocks = jnp.where(col_in, _toeplitz(rpb) * LOG2E, NEG_INF)
    outside = jnp.full((NA_HEADS, GRID_W, GRID_W), NEG_INF, F32)
    n_k, n_q = NA_TILE_BAND * GRID_W, NA_TILE_ROWS * GRID_W
    tabs = []
    for t in NA_VARIANT_TILES:
        r0, b0 = _na_tile_band(t)
        key_rows = []
        for ka in range(b0, b0 + NA_TILE_BAND):
            per_q = []
            for r in range(r0, r0 + NA_TILE_ROWS):
                rs = min(max(r - NA_WIN_ROWS // 2, 0), GRID_ROWS - NA_WIN_ROWS)
                per_q.append(blocks[:, ka - r + NA_WIN_ROWS - 1] if rs <= ka < rs + NA_WIN_ROWS else outside)
            key_rows.append(jnp.concatenate(per_q, axis=-1))
        b = jnp.concatenate(key_rows, axis=1).reshape(NA_HEADS // 2, 2, n_k, n_q)
        tabs.append(jnp.transpose(b, (0, 2, 1, 3)).reshape(NA_HEADS // 2, n_k, 2 * n_q))
    return jnp.stack(tabs).astype(F32)


def _na_attention(qkv, bias):
    n_hb = NA_HEADS // 2
    kb, vb = D // LANES, 2 * D // LANES
    in_specs = [pl.BlockSpec((SEQ, LANES), lambda h, b: (b, h)),
                pl.BlockSpec((SEQ, LANES), lambda h, b: (b, kb + h)),
                pl.BlockSpec((SEQ, LANES), lambda h, b: (b, vb + h)),
                pl.BlockSpec((CTX, LANES), lambda h, b: (N_LAT // CTX + b, kb + h)),
                pl.BlockSpec((CTX, LANES), lambda h, b: (N_LAT // CTX + b, vb + h)),
                pl.BlockSpec((len(NA_VARIANT_TILES), None) + bias.shape[2:], lambda h, b: (0, h, 0, 0))]
    return pl.pallas_call(
        _na_attn_kernel,
        grid=(n_hb, BATCH),
        in_specs=in_specs,
        out_specs=pl.BlockSpec((SEQ, LANES), lambda h, b: (b, h)),
        out_shape=jax.ShapeDtypeStruct((N_LAT, D), BF16),
        scratch_shapes=[pltpu.VMEM((2, NA_TILE_BAND * GRID_W + CTX, CHAIN_N), F32)],
        compiler_params=_params("arbitrary", "arbitrary"),
        name="na_attention",
    )(qkv, qkv, qkv, qkv, qkv, bias)


def _dup_heads(w, n_heads):
    w = w.reshape(D, n_heads, 1, HEAD)
    return jnp.broadcast_to(w, (D, n_heads, LANES // HEAD, HEAD)).reshape(D, n_heads * LANES)


def _pad_heads(w, width):
    k = w.shape[0]
    w = w.reshape(k, MLA_HEADS, width)
    return jnp.pad(w, ((0, 0), (0, 0), (0, LANES - width))).reshape(k, MLA_HEADS * LANES)


def kernel(x, c, ctx, c_ctx, w_mod, b_mod, norm_g, w_ffn_in, w_ffn_out, da_w_qkv, da_lam_q1, da_lam_k1, da_lam_q2, da_lam_k2, da_subln_g, da_w_o, gqa_w_qkv, gqa_q_norm_g, gqa_k_norm_g, gqa_w_o, mla_w_down, mla_q_norm_g, mla_kv_norm_g, mla_w_uq, mla_w_ukv, mla_w_o, na_w_qkv, na_rpb, na_w_o, final_g):
    assert x.shape == (BATCH, SEQ, D) and ctx.shape == (BATCH, CTX, D) and w_mod.shape[0] == DEPTH == 4

    cvec = jnp.zeros((MOD_ROWS, D), F32).at[:BATCH].set(c).at[BATCH].set(c_ctx)
    mod = _modulation(cvec, w_mod, b_mod).reshape(DEPTH, MOD_ROWS, N_MOD, D)
    xs = jnp.concatenate([x.reshape(N_LAT, D), ctx.reshape(BATCH * CTX, D)], axis=0)
    w_in, w_out = w_ffn_in.astype(BF16), w_ffn_out.astype(BF16)

    rope64 = _rope_tables(HEAD, 0, HEAD)
    rope_mla = _rope_tables(MLA_ROPE, MLA_NOPE, LANES)
    ones_blk = jnp.kron(jnp.eye(LANES // HEAD, dtype=F32), jnp.ones((HEAD, HEAD), F32)).astype(BF16)
    blk_spec = _resident((LANES, LANES), lambda i: (0, 0))

    for layer in range(DEPTH):
        last = layer == DEPTH - 1
        xs = _ffn(xs, mod, layer, norm_g[layer, 0], w_in[layer, 0], w_out[layer, 0], mod_row=0, n_rows=N_TOK)
        g_mix = norm_g[layer, 1]
        if layer == 0:
            qkv = _project(_da_proj_kernel, xs, mod, layer, g_mix, da_w_qkv[0].astype(BF16),
                           list(rope64), _rope_specs(), 3 * D, "da_proj")
            lam_init = 0.8 - 0.6 * math.exp(-0.3 * layer)
            lam = jnp.zeros((8, LANES), F32).at[:4, :HEAD].set(
                jnp.stack([da_lam_q1[0], da_lam_k1[0], da_lam_q2[0], da_lam_k2[0]]))
            const3 = lambda *_: (0, 0)
            o = _attention(functools.partial(_da_attn_kernel, lam_init=lam_init), qkv,
                           LANES, LANES, LANES, LANES, D, 2 * D, DA_HEADS,
                           extra=(lam, _row(da_subln_g[0])),
                           extra_specs=(pl.BlockSpec((8, LANES), const3), pl.BlockSpec((1, LANES), const3)),
                           name="da_attention")
            w_o = da_w_o[0]
        elif layer == 1:
            wq, wk, wv = jnp.split(gqa_w_qkv[0], [GQA_HEADS * HEAD, (GQA_HEADS + GQA_KV) * HEAD], axis=1)
            w = jnp.concatenate([wq, _dup_heads(wk, GQA_KV), _dup_heads(wv, GQA_KV)], axis=1).astype(BF16)
            gains = (_row(jnp.tile(gqa_q_norm_g[0], LANES // HEAD)), _row(jnp.tile(gqa_k_norm_g[0], LANES // HEAD)))
            qkv = _project(_gqa_proj_kernel, xs, mod, layer, g_mix, w,
                           list(rope64) + list(gains) + [ones_blk],
                           _rope_specs() + [_row_spec(LANES)] * 2 + [blk_spec],
                           D + 2 * GQA_KV * LANES, "gqa_proj")
            o = _attention(functools.partial(_pair_attn_kernel, wide_keys=False), qkv,
                           2 * LANES, LANES, LANES, 2 * LANES, D, D + GQA_KV * LANES, GQA_KV,
                           name="gqa_attention")
            w_o = gqa_w_o[0]
        elif layer == 2:
            wd = mla_w_down[0]
            n_lora = MLA_Q_LORA + MLA_KV_LORA
            wd = jnp.concatenate([wd[:, :n_lora], jnp.zeros((D, MLA_NOPE), F32), wd[:, n_lora:],
                                  jnp.zeros((D, LANES - MLA_NOPE - MLA_ROPE), F32)], axis=1).astype(BF16)
            wuq = _pad_heads(mla_w_uq[0], MLA_NOPE + MLA_ROPE).astype(BF16)
            wukv = mla_w_ukv[0].reshape(MLA_KV_LORA, MLA_HEADS, 2 * HEAD)
            wuk = _pad_heads(wukv[:, :, :MLA_NOPE].reshape(MLA_KV_LORA, -1), MLA_NOPE)
            wuv = wukv[:, :, MLA_NOPE:].reshape(MLA_KV_LORA, MLA_HEADS * HEAD)
            wukv = jnp.concatenate([wuk, wuv], axis=1).astype(BF16)
            n_q = MLA_HEADS * LANES
            const = lambda i: (0, 0)
            qkv = _project(_mla_proj_kernel, xs, mod, layer, g_mix, wd,
                           list(rope_mla) + [_row(mla_q_norm_g[0]), _row(mla_kv_norm_g[0]), wuq, wukv],
                           _rope_specs() + [_row_spec(MLA_Q_LORA), _row_spec(MLA_KV_LORA),
                                            _resident(wuq.shape, const), _resident(wukv.shape, const)],
                           2 * n_q + D, "mla_proj")
            o = _attention(functools.partial(_pair_attn_kernel, wide_keys=True), qkv,
                           2 * LANES, 2 * LANES, LANES, LANES, n_q, 2 * n_q, MLA_HEADS // 2,
                           name="mla_attention")
            w_o = mla_w_o[0]
        else:
            qkv = _project(_na_proj_kernel, xs, mod, layer, g_mix, na_w_qkv[0].astype(BF16), [], [], 3 * D, "na_proj")
            o = _na_attention(qkv, _na_bias(na_rpb[0]))
            w_o = na_w_o[0]
        xs = _ffn(xs, mod, layer, norm_g[layer, 2], w_in[layer, 1], w_out[layer, 1], mod_row=6,
                  n_rows=N_LAT if last else N_TOK, attn=(o, w_o.astype(BF16)),
                  final_g=final_g if last else None)
    return xs.reshape(BATCH, SEQ, D)
```

```python
import collections
import functools
import math

import jax
import jax.numpy as jnp
import numpy as np
from jax import lax
from jax.experimental import pallas as pl
from jax.experimental.pallas import tpu as pltpu

F32 = jnp.float32
BF16 = jnp.bfloat16

D = 1024
BATCH = 8
SEQ = 2048
DEPTH = 4
GRID_W = 64
GRID_ROWS = SEQ // GRID_W
CTX = 256
N_MOD = 9
D_FF = 2816
ROPE_THETA = 10000.0
NEG_INF = -1e30
EPS = 1e-6
LOG2E = math.log2(math.e)

N_LAT = BATCH * SEQ
N_TOK = N_LAT + BATCH * CTX
N_GROUPS = N_TOK // SEQ
MOD_ROWS = 16
assert BATCH * CTX == SEQ and N_GROUPS == BATCH + 1 and N_GROUPS <= MOD_ROWS

LANES = 128
MXU_N = 256
HEAD = 64
FF_CHUNKS = D_FF // MXU_N
assert FF_CHUNKS * MXU_N == D_FF

DA_HEADS = 8
GQA_HEADS, GQA_KV = 16, 4
MLA_HEADS, MLA_Q_LORA, MLA_KV_LORA, MLA_NOPE, MLA_ROPE = 16, 256, 128, 64, 32
NA_HEADS, NA_WIN_ROWS, NA_WIN_COLS = 16, 8, 16
NA_TILE_ROWS = 4
NA_TILE_BAND = 12
NA_TILES = GRID_ROWS // NA_TILE_ROWS
NA_VARIANT_TILES = (0, 1, NA_TILES - 1)
NA_TILE_VARIANT = tuple(0 if t == 0 else 2 if t == NA_TILES - 1 else 1 for t in range(NA_TILES))
assert NA_TILE_ROWS + NA_WIN_ROWS <= NA_TILE_BAND and (NA_TILE_BAND * GRID_W) % MXU_N == 0

TM = 512
CHAINS_PER_STEP = 8
SUB_Q = 256
CHAIN_N = 2 * SUB_Q
ATT_KC = 1024
MOD_TN = 1024
PROJ_TN = 512
LAT_TILES = SEQ // TM
assert SUB_Q == CTX == NA_TILE_ROWS * GRID_W and SEQ % (CHAINS_PER_STEP * SUB_Q) == 0
VMEM_LIMIT = 56 * 1024 * 1024

NT = (((1,), (1,)), ((), ()))
TN = (((0,), (0,)), ((), ()))


def _params(*sem):
    return pltpu.CompilerParams(dimension_semantics=sem, vmem_limit_bytes=VMEM_LIMIT)


def _resident(shape, index_map):
    return pl.BlockSpec(shape, index_map, pipeline_mode=pl.Buffered(1))


def _bdot(a, b):
    return jnp.dot(a, b, preferred_element_type=F32)


def _rms(x):
    return x * lax.rsqrt(jnp.mean(x * x, axis=-1, keepdims=True) + EPS)


def _modulated(x, g, shift, scale):
    return (_rms(x) * g) * (1.0 + scale) + shift


def _mod_kernel(c_ref, w_ref, b_ref, o_ref):
    c = c_ref[...]
    s = (c * jax.nn.sigmoid(c)).astype(BF16)
    o_ref[...] = _bdot(s, w_ref[...].astype(BF16)) + b_ref[...]


def _modulation(cvec, w_mod, b_mod):
    n = N_MOD * D
    return pl.pallas_call(
        _mod_kernel,
        grid=(DEPTH, n // MOD_TN),
        in_specs=[pl.BlockSpec((MOD_ROWS, D), lambda l, j: (0, 0)),
                  pl.BlockSpec((None, D, MOD_TN), lambda l, j: (l, 0, j)),
                  pl.BlockSpec((None, 1, MOD_TN), lambda l, j: (l, 0, j))],
        out_specs=pl.BlockSpec((None, MOD_ROWS, MOD_TN), lambda l, j: (l, 0, j)),
        out_shape=jax.ShapeDtypeStruct((DEPTH, MOD_ROWS, n), F32),
        compiler_params=_params("arbitrary", "arbitrary"),
        name="modulation",
    )(cvec, w_mod, b_mod.reshape(DEPTH, 1, n))


def _row_specs(src):
    if not isinstance(src, tuple):
        return [pl.BlockSpec((TM, src.shape[1]), lambda i: (i, 0))], [src]
    lat, ctx = src
    n_lat = lat.shape[0] // TM
    return ([pl.BlockSpec((TM, lat.shape[1]), lambda i: (jnp.minimum(i, n_lat - 1), 0)),
             pl.BlockSpec((TM, ctx.shape[1]), lambda i: (jnp.maximum(i - n_lat, 0), 0))], [lat, ctx])


def _read_rows(refs):
    if len(refs) == 1:
        return refs[0][...]
    return jnp.where(pl.program_id(0) < N_LAT // TM, refs[0][...], refs[1][...])


def _ffn_kernel(*refs, n_x, n_o, mod_row, final_norm):
    it = iter(refs)
    x_refs = [next(it) for _ in range(n_x)]
    mod_ref, g_ref, win_ref, wout_ref = (next(it) for _ in range(4))
    o_refs = [next(it) for _ in range(n_o)]
    wo_ref = next(it) if n_o else None
    fg_ref = next(it) if final_norm else None
    out_ref, a_scr = next(it), next(it)

    x = _read_rows(x_refs)
    mod = mod_ref[...]
    if n_o:
        x = x + mod[5:6] * _bdot(_read_rows(o_refs), wo_ref[...])
    shift, scale, gate = (mod[mod_row + k:mod_row + k + 1] for k in range(3))
    h = _modulated(x, g_ref[...], shift, scale).astype(BF16)
    for c in range(FF_CHUNKS):
        lo = c * MXU_N
        gt = _bdot(h, win_ref[:, lo:lo + MXU_N])
        up = _bdot(h, win_ref[:, D_FF + lo:D_FF + lo + MXU_N])
        a_scr[:, lo:lo + MXU_N] = ((gt * jax.nn.sigmoid(gt)) * up).astype(BF16)
    y = x + (0.5 * gate) * _bdot(a_scr[...], wout_ref[...])
    if final_norm:
        y = _rms(y) * fg_ref[...]
    out_ref[...] = y


def _ffn(x, mod, layer, which, g, w_in, w_out, *, n_rows, attn=None, final_g=None):
    const = lambda i: (0, 0)
    x_specs, x_args = _row_specs(x)
    in_specs = x_specs + [pl.BlockSpec((None, None, N_MOD, D), lambda i: (layer, i // LAT_TILES, 0, 0)),
                          _resident((1, D), const),
                          _resident((None, None, D, 2 * D_FF), lambda i: (layer, which, 0, 0)),
                          _resident((None, None, D_FF, D), lambda i: (layer, which, 0, 0))]
    args = x_args + [mod, g.reshape(1, D), w_in, w_out]
    n_o = 0
    if attn is not None:
        o, w_o = attn
        o_specs, o_args = _row_specs(o)
        n_o = len(o_args)
        in_specs += o_specs + [_resident((D, D), const)]
        args += o_args + [w_o]
    if final_g is not None:
        in_specs.append(_resident((1, D), const))
        args.append(final_g.reshape(1, D))
    return pl.pallas_call(
        functools.partial(_ffn_kernel, n_x=len(x_args), n_o=n_o, mod_row=6 * which,
                          final_norm=final_g is not None),
        grid=(n_rows // TM,),
        in_specs=in_specs,
        out_specs=pl.BlockSpec((TM, D), lambda i: (i, 0)),
        out_shape=jax.ShapeDtypeStruct((n_rows, D), F32),
        scratch_shapes=[pltpu.VMEM((TM, D_FF), BF16)],
        compiler_params=_params("arbitrary"),
        name="ffn",
    )(*args)


def _rope_tables(rd, lane0, period):
    t = jnp.arange(SEQ)
    half, nf = rd // 2, rd // 4
    inv_freq = ROPE_THETA ** (-jnp.arange(nf, dtype=F32) / nf)
    ang_r = (t // GRID_W).astype(F32)[:, None] * inv_freq
    ang_c = (t % GRID_W).astype(F32)[:, None] * inv_freq
    zero = jnp.zeros((SEQ, nf), F32)
    cos = jnp.concatenate([jnp.cos(ang_r)] * 2 + [jnp.cos(ang_c)] * 2, axis=1)
    s_up = jnp.concatenate([-jnp.sin(ang_r), zero, -jnp.sin(ang_c), zero], axis=1)
    s_dn = jnp.concatenate([zero, jnp.sin(ang_r), zero, jnp.sin(ang_c)], axis=1)
    assert half == 2 * nf and LANES % period == 0 and lane0 + rd <= period

    def place(blk, fill):
        pat = jnp.full((SEQ, period), fill, F32).at[:, lane0:lane0 + rd].set(blk)
        pat = jnp.tile(pat, (1, LANES // period))
        return jnp.concatenate([pat, jnp.full((TM, LANES), fill, F32)], axis=0)

    return place(cos, 1.0), place(s_up, 0.0), place(s_dn, 0.0)


def _rope(x, cos, s_up, s_dn, q):
    return x * cos + pltpu.roll(x, LANES - q, 1) * s_up + pltpu.roll(x, q, 1) * s_dn


def _rope_specs():
    m = lambda i: (jnp.where(i < BATCH * LAT_TILES, i % LAT_TILES, LAT_TILES), 0)
    return [pl.BlockSpec((TM, LANES), m)] * 3


def _blocks(n_cols):
    return range(0, n_cols, LANES)


def _head_sumsq(x, ones_blk):
    xx = x * x
    hi = xx.astype(BF16)
    lo = (xx - hi.astype(F32)).astype(BF16)
    return _bdot(hi, ones_blk) + _bdot(lo, ones_blk)


def _mixer_input(x_ref, mod_ref, g_ref):
    mod = mod_ref[...]
    return _modulated(x_ref[...], g_ref[...], mod[3:4], mod[4:5]).astype(BF16)


def _qk_tables(c_ref, su_ref, sd_ref):
    k_tabs = (c_ref[...], su_ref[...], sd_ref[...])
    return tuple(t * (HEAD ** -0.5 * LOG2E) for t in k_tabs), k_tabs


def _da_proj_kernel(x_ref, mod_ref, g_ref, w_ref, c_ref, su_ref, sd_ref, out_ref):
    h = _mixer_input(x_ref, mod_ref, g_ref)
    q_tabs, k_tabs = _qk_tables(c_ref, su_ref, sd_ref)
    for c0 in range(0, 2 * D, PROJ_TN):
        y = _bdot(h, w_ref[:, c0:c0 + PROJ_TN])
        for lo in _blocks(PROJ_TN):
            tabs = q_tabs if c0 < D else k_tabs
            out_ref[:, c0 + lo:c0 + lo + LANES] = _rope(y[:, lo:lo + LANES], *tabs, HEAD // 4).astype(BF16)
    out_ref[:, 2 * D:] = _bdot(h, w_ref[:, 2 * D:]).astype(BF16)


def _gqa_proj_kernel(x_ref, mod_ref, g_ref, w_ref, c_ref, su_ref, sd_ref, qg_ref, kg_ref, ones_ref, out_ref):
    h = _mixer_input(x_ref, mod_ref, g_ref)
    q_tabs, k_tabs = _qk_tables(c_ref, su_ref, sd_ref)
    ones_blk = ones_ref[...]
    n_qk = D + GQA_KV * LANES
    for c0 in range(0, n_qk, PROJ_TN):
        y = _bdot(h, w_ref[:, c0:c0 + PROJ_TN])
        tabs, gain = (q_tabs, qg_ref[...]) if c0 < D else (k_tabs, kg_ref[...])
        for m0 in range(0, PROJ_TN, MXU_N):
            ym = y[:, m0:m0 + MXU_N]
            ym = (ym * lax.rsqrt(_head_sumsq(ym, ones_blk) * (1.0 / HEAD) + EPS)) * gain
            for lo in _blocks(MXU_N):
                out_ref[:, c0 + m0 + lo:c0 + m0 + lo + LANES] = _rope(
                    ym[:, lo:lo + LANES], *tabs, HEAD // 4).astype(BF16)
    out_ref[:, n_qk:] = _bdot(h, w_ref[:, n_qk:]).astype(BF16)


def _mla_proj_kernel(x_ref, mod_ref, g_ref, wd_ref, c_ref, su_ref, sd_ref, qg_ref, kvg_ref,
                     wuq_ref, wukv_ref, out_ref):
    h = _mixer_input(x_ref, mod_ref, g_ref)
    c, su, sd = c_ref[...], su_ref[...], sd_ref[...]
    d = _bdot(h, wd_ref[...])
    cq = (_rms(d[:, :MLA_Q_LORA]) * qg_ref[...]).astype(BF16)
    ckv = (_rms(d[:, MLA_Q_LORA:MLA_Q_LORA + MLA_KV_LORA]) * kvg_ref[...]).astype(BF16)
    k_pe = _rope(d[:, MLA_Q_LORA + MLA_KV_LORA:], c, su, sd, MLA_ROPE // 4)
    n_q = MLA_HEADS * LANES
    for c0 in range(0, n_q, PROJ_TN):
        q = _bdot(cq, wuq_ref[:, c0:c0 + PROJ_TN])
        k = _bdot(ckv, wukv_ref[:, c0:c0 + PROJ_TN])
        for lo in _blocks(PROJ_TN):
            qh = _rope(q[:, lo:lo + LANES], c, su, sd, MLA_ROPE // 4) * ((MLA_NOPE + MLA_ROPE) ** -0.5 * LOG2E)
            out_ref[:, c0 + lo:c0 + lo + LANES] = qh.astype(BF16)
            out_ref[:, n_q + c0 + lo:n_q + c0 + lo + LANES] = (k[:, lo:lo + LANES] + k_pe).astype(BF16)
    out_ref[:, 2 * n_q:] = _bdot(ckv, wukv_ref[:, n_q:]).astype(BF16)


def _na_proj_kernel(x_ref, mod_ref, g_ref, w_ref, out_ref):
    h = _mixer_input(x_ref, mod_ref, g_ref)
    out_ref[:, :D] = (_bdot(h, w_ref[:, :D]) * (HEAD ** -0.5 * LOG2E)).astype(BF16)
    out_ref[:, D:] = _bdot(h, w_ref[:, D:]).astype(BF16)


def _project(kernel, x, mod, layer, g, w, extra, extra_specs, out_cols, name):
    const = lambda i: (0, 0)
    in_specs = [pl.BlockSpec((TM, D), lambda i: (i, 0)),
                pl.BlockSpec((None, None, N_MOD, D), lambda i: (layer, i // LAT_TILES, 0, 0)),
                _resident((1, D), const),
                _resident(w.shape, const)] + extra_specs
    return pl.pallas_call(
        kernel,
        grid=(N_TOK // TM,),
        in_specs=in_specs,
        out_specs=pl.BlockSpec((TM, out_cols), lambda i: (i, 0)),
        out_shape=jax.ShapeDtypeStruct((N_TOK, out_cols), BF16),
        compiler_params=_params("arbitrary"),
        name=name,
    )(x, mod, g.reshape(1, D), w, *extra)


def _row(v):
    return v.reshape(1, -1).astype(F32)


def _row_spec(n):
    return _resident((1, n), lambda i: (0, 0))


def _stack_halves(q):
    lane = lax.broadcasted_iota(jnp.int32, (1, LANES), 1)
    m0 = (lane < HEAD).astype(F32)
    qf = q.astype(F32)
    return jnp.concatenate([(qf * m0).astype(BF16), (qf * (1.0 - m0)).astype(BF16)], axis=0)


def _stack_wide(q):
    qa, qb = q[:, :LANES], q[:, LANES:]
    zero = jnp.zeros_like(qa)
    return jnp.concatenate([jnp.concatenate([qa, zero], axis=1), jnp.concatenate([zero, qb], axis=1)], axis=0)


def _pair_out(o_t):
    rows = o_t.shape[1] // 2
    return jnp.concatenate([o_t[:HEAD, :rows], o_t[HEAD:, rows:]], axis=0).T


_Chain = collections.namedtuple("_Chain", "qs segs bias emit")


def _run_chains(chains, s_scr):
    chunks = [[(k, v, lo + c, min(ATT_KC, n - c)) for k, v, lo, n in ch.segs for c in range(0, n, ATT_KC)]
              for ch in chains]
    offs = [[sum(n for _, _, _, n in cl[:j]) for j in range(len(cl))] for cl in chunks]
    stats = [dict(m=None, l=None, acc=None) for _ in chains]

    def score(i, j):
        ch, st = chains[i], stats[i]
        k_ref, _, lo, n = chunks[i][j]
        s = lax.dot_general(k_ref[lo:lo + n, :], ch.qs, NT, preferred_element_type=F32)
        b = ch.bias(offs[i][j], n) if ch.bias is not None else None
        if b is not None:
            s = s + b
        s_scr[i % 2, offs[i][j]:offs[i][j] + n, :] = s
        mj = jnp.max(s, axis=0, keepdims=True)
        st["m"] = mj if st["m"] is None else jnp.maximum(st["m"], mj)

    def value(i, j):
        st = stats[i]
        _, v_ref, lo, n = chunks[i][j]
        e = jnp.exp2(s_scr[i % 2, offs[i][j]:offs[i][j] + n, :] - st["m"])
        lj = jnp.sum(e, axis=0, keepdims=True)
        pv = lax.dot_general(v_ref[lo:lo + n, :], e.astype(BF16), TN, preferred_element_type=F32)
        st["l"] = lj if st["l"] is None else st["l"] + lj
        st["acc"] = pv if st["acc"] is None else st["acc"] + pv

    for i in range(len(chains) + 1):
        n_score = len(chunks[i]) if i < len(chains) else 0
        n_value = len(chunks[i - 1]) if i > 0 else 0
        for j in range(max(n_score, n_value)):
            if j < n_score:
                score(i, j)
            if j < n_value:
                value(i - 1, j)
        if i > 0:
            chains[i - 1].emit(stats[i - 1]["acc"] / stats[i - 1]["l"])


def _attn_refs(refs, n_seg, n_extra):
    kv = refs[1:1 + 2 * n_seg]
    segs = [(kv[2 * s], kv[2 * s + 1], 0, kv[2 * s].shape[0]) for s in range(n_seg)]
    return refs[0], segs, refs[1 + 2 * n_seg:1 + 2 * n_seg + n_extra], refs[-2], refs[-1]


def _attn_kernel(*refs, build, n_seg, n_extra, head_cols):
    q_ref, segs, extras, o_ref, s_scr = _attn_refs(refs, n_seg, n_extra)
    if head_cols is None:
        chains = build(q_ref, segs, extras, o_ref)
    else:
        qc, kc, vc, oc = head_cols
        chains = []
        for hb in range(q_ref.shape[1] // qc):
            view = lambda ref, w: ref.at[:, hb * w:(hb + 1) * w]
            segs_h = [(view(k, kc), view(v, vc), lo, n) for k, v, lo, n in segs]
            chains += build(view(q_ref, qc), segs_h, extras, view(o_ref, oc))
    _run_chains(chains, s_scr)


def _da_chains(q_ref, segs, extras, o_ref, *, lam_init):
    lam_ref, g_ref = extras
    lv = lam_ref[...]
    lam = (jnp.exp(jnp.sum(lv[0:1] * lv[1:2], axis=-1, keepdims=True))
           - jnp.exp(jnp.sum(lv[2:3] * lv[3:4], axis=-1, keepdims=True)) + lam_init)

    def emit(rows):
        def f(o_t):
            o = o_t[:, :SUB_Q] - lam * o_t[:, SUB_Q:]
            o = o * lax.rsqrt(jnp.mean(o * o, axis=0, keepdims=True) + EPS)
            o_ref[rows, :] = ((o.T * g_ref[...]) * (1.0 - lam_init)).astype(BF16)
        return f

    chains = []
    for lo in range(0, q_ref.shape[0], SUB_Q):
        rows = slice(lo, lo + SUB_Q)
        chains.append(_Chain(_stack_halves(q_ref[rows, :]), segs, None, emit(rows)))
    return chains


def _pair_chains(q_ref, segs, extras, o_ref, *, wide_keys):
    def emit(rows, cols):
        def f(o_t):
            o_ref[rows, cols] = _pair_out(o_t).astype(BF16)
        return f

    chains = []
    for lo in range(0, q_ref.shape[0], SUB_Q):
        rows = slice(lo, lo + SUB_Q)
        if wide_keys:
            chains.append(_Chain(_stack_wide(q_ref[rows, :]), segs, None, emit(rows, slice(0, LANES))))
        else:
            for c in _blocks(q_ref.shape[1]):
                cols = slice(c, c + LANES)
                chains.append(_Chain(_stack_halves(q_ref[rows, cols]), segs, None, emit(rows, cols)))
    return chains


def _attention(build, qkv, q_cols, k_cols, v_cols, o_cols, k0, v0, n_hb, extra=(), extra_specs=(), name=""):
    kb, vb = k0 // k_cols, v0 // v_cols
    ctx_blk = N_LAT // CTX
    scratch = lambda n_keys: [pltpu.VMEM((2, n_keys, CHAIN_N), F32)]
    tq = CHAINS_PER_STEP * SUB_Q // (o_cols // LANES)
    n_qt = SEQ // tq
    kernel = functools.partial(_attn_kernel, build=build, n_extra=len(extra))
    lat = pl.pallas_call(
        functools.partial(kernel, n_seg=2, head_cols=None),
        grid=(BATCH, n_hb, n_qt),
        in_specs=[pl.BlockSpec((tq, q_cols), lambda b, h, t: (b * n_qt + t, h)),
                  pl.BlockSpec((SEQ, k_cols), lambda b, h, t: (b, kb + h)),
                  pl.BlockSpec((SEQ, v_cols), lambda b, h, t: (b, vb + h)),
                  pl.BlockSpec((CTX, k_cols), lambda b, h, t: (ctx_blk + b, kb + h)),
                  pl.BlockSpec((CTX, v_cols), lambda b, h, t: (ctx_blk + b, vb + h))] + list(extra_specs),
        out_specs=pl.BlockSpec((tq, o_cols), lambda b, h, t: (b * n_qt + t, h)),
        out_shape=jax.ShapeDtypeStruct((N_LAT, D), BF16),
        scratch_shapes=scratch(SEQ + CTX),
        compiler_params=_params("arbitrary", "arbitrary", "arbitrary"),
        name=name,
    )(qkv, qkv, qkv, qkv, qkv, *extra)
    assert n_hb * o_cols == D and k0 % (n_hb * k_cols) == 0 and v0 % (n_hb * v_cols) == 0
    ctx = pl.pallas_call(
        functools.partial(kernel, n_seg=1, head_cols=(q_cols, k_cols, v_cols, o_cols)),
        grid=(BATCH,),
        in_specs=[pl.BlockSpec((CTX, n_hb * q_cols), lambda b: (ctx_blk + b, 0)),
                  pl.BlockSpec((CTX, n_hb * k_cols), lambda b: (ctx_blk + b, kb // n_hb)),
                  pl.BlockSpec((CTX, n_hb * v_cols), lambda b: (ctx_blk + b, vb // n_hb))] + list(extra_specs),
        out_specs=pl.BlockSpec((CTX, D), lambda b: (b, 0)),
        out_shape=jax.ShapeDtypeStruct((BATCH * CTX, D), BF16),
        scratch_shapes=scratch(CTX),
        compiler_params=_params("arbitrary"),
        name=name + "_ctx",
    )(qkv, qkv, qkv, *extra)
    return lat, ctx


def _na_tile_band(t):
    r0 = t * NA_TILE_ROWS
    return r0, min(max(r0 - NA_WIN_ROWS // 2, 0), GRID_ROWS - NA_TILE_BAND)


def _na_attn_kernel(q_ref, kl_ref, vl_ref, kc_ref, vc_ref, bias_ref, o_ref, s_scr):
    n_band = NA_TILE_BAND * GRID_W

    def emit(rows):
        def f(o_t):
            o_ref[rows, :] = _pair_out(o_t).astype(BF16)
        return f

    def band_bias(variant):
        return lambda off, n: bias_ref[variant, off:off + n, :] if off + n <= n_band else None

    chains = []
    for t in range(NA_TILES):
        r0, b0 = _na_tile_band(t)
        rows = slice(r0 * GRID_W, (r0 + NA_TILE_ROWS) * GRID_W)
        segs = [(kl_ref, vl_ref, b0 * GRID_W, n_band), (kc_ref, vc_ref, 0, CTX)]
        chains.append(_Chain(_stack_halves(q_ref[rows, :]), segs, band_bias(NA_TILE_VARIANT[t]), emit(rows)))
    _run_chains(chains, s_scr)


def _toeplitz(v):
    lead, w, period = v.shape[:-1], NA_WIN_COLS, 2 * GRID_W - 1
    line = jnp.zeros(lead + (period,), v.dtype)
    line = line.at[..., :w].set(v[..., w - 1:]).at[..., period - (w - 1):].set(v[..., :w - 1])
    x = jnp.broadcast_to(line[..., None, :], lead + (GRID_W, period)).reshape(lead + (GRID_W * period,))
    skew = x[..., :GRID_W * (period - 1)].reshape(lead + (GRID_W, period - 1))
    return jnp.swapaxes(skew[..., :GRID_W], -1, -2)


def _na_bias(rpb):
    cols = np.arange(GRID_W)
    col_start = np.clip(cols - NA_WIN_COLS // 2, 0, GRID_W - NA_WIN_COLS)
    col_in = (cols[:, None] >= col_start[None, :]) & (cols[:, None] < col_start[None, :] + NA_WIN_COLS)
    blocks = jnp.where(col_in, _toeplitz(rpb) * LOG2E, NEG_INF)
    outside = jnp.full((NA_HEADS, GRID_W, GRID_W), NEG_INF, F32)
    n_k, n_q = NA_TILE_BAND * GRID_W, NA_TILE_ROWS * GRID_W
    tabs = []
    for t in NA_VARIANT_TILES:
        r0, b0 = _na_tile_band(t)
        key_rows = []
        for ka in range(b0, b0 + NA_TILE_BAND):
            per_q = []
            for r in range(r0, r0 + NA_TILE_ROWS):
                rs = min(max(r - NA_WIN_ROWS // 2, 0), GRID_ROWS - NA_WIN_ROWS)
                per_q.append(blocks[:, ka - r + NA_WIN_ROWS - 1] if rs <= ka < rs + NA_WIN_ROWS else outside)
            key_rows.append(jnp.concatenate(per_q, axis=-1))
        b = jnp.concatenate(key_rows, axis=1).reshape(NA_HEADS // 2, 2, n_k, n_q)
        tabs.append(jnp.transpose(b, (0, 2, 1, 3)).reshape(NA_HEADS // 2, n_k, 2 * n_q))
    return jnp.stack(tabs).astype(F32)


def _na_attention(qkv, bias):
    n_hb = NA_HEADS // 2
    kb, vb = D // LANES, 2 * D // LANES
    in_specs = [pl.BlockSpec((SEQ, LANES), lambda h, b: (b, h)),
                pl.BlockSpec((SEQ, LANES), lambda h, b: (b, kb + h)),
                pl.BlockSpec((SEQ, LANES), lambda h, b: (b, vb + h)),
                pl.BlockSpec((CTX, LANES), lambda h, b: (N_LAT // CTX + b, kb + h)),
                pl.BlockSpec((CTX, LANES), lambda h, b: (N_LAT // CTX + b, vb + h)),
                pl.BlockSpec((len(NA_VARIANT_TILES), None) + bias.shape[2:], lambda h, b: (0, h, 0, 0))]
    return pl.pallas_call(
        _na_attn_kernel,
        grid=(n_hb, BATCH),
        in_specs=in_specs,
        out_specs=pl.BlockSpec((SEQ, LANES), lambda h, b: (b, h)),
        out_shape=jax.ShapeDtypeStruct((N_LAT, D), BF16),
        scratch_shapes=[pltpu.VMEM((2, NA_TILE_BAND * GRID_W + CTX, CHAIN_N), F32)],
        compiler_params=_params("arbitrary", "arbitrary"),
        name="na_attention",
    )(qkv, qkv, qkv, qkv, qkv, bias)


def _dup_heads(w, n_heads):
    w = w.reshape(D, n_heads, 1, HEAD)
    return jnp.broadcast_to(w, (D, n_heads, LANES // HEAD, HEAD)).reshape(D, n_heads * LANES)


def _pad_heads(w, width):
    k = w.shape[0]
    w = w.reshape(k, MLA_HEADS, width)
    return jnp.pad(w, ((0, 0), (0, 0), (0, LANES - width))).reshape(k, MLA_HEADS * LANES)


def kernel(x, c, ctx, c_ctx, w_mod, b_mod, norm_g, w_ffn_in, w_ffn_out, da_w_qkv, da_lam_q1, da_lam_k1, da_lam_q2, da_lam_k2, da_subln_g, da_w_o, gqa_w_qkv, gqa_q_norm_g, gqa_k_norm_g, gqa_w_o, mla_w_down, mla_q_norm_g, mla_kv_norm_g, mla_w_uq, mla_w_ukv, mla_w_o, na_w_qkv, na_rpb, na_w_o, final_g):
    assert x.shape == (BATCH, SEQ, D) and ctx.shape == (BATCH, CTX, D) and w_mod.shape[0] == DEPTH == 4

    cvec = jnp.zeros((MOD_ROWS, D), F32).at[:BATCH].set(c).at[BATCH].set(c_ctx)
    mod = _modulation(cvec, w_mod, b_mod).reshape(DEPTH, MOD_ROWS, N_MOD, D)
    xs = (x.reshape(N_LAT, D), ctx.reshape(BATCH * CTX, D))
    w_in, w_out = w_ffn_in.astype(BF16), w_ffn_out.astype(BF16)

    rope64 = _rope_tables(HEAD, 0, HEAD)
    rope_mla = _rope_tables(MLA_ROPE, MLA_NOPE, LANES)
    ones_blk = jnp.kron(jnp.eye(MXU_N // HEAD, dtype=F32), jnp.ones((HEAD, HEAD), F32)).astype(BF16)
    blk_spec = _resident((MXU_N, MXU_N), lambda i: (0, 0))

    for layer in range(DEPTH):
        last = layer == DEPTH - 1
        xs = _ffn(xs, mod, layer, 0, norm_g[layer, 0], w_in, w_out, n_rows=N_TOK)
        g_mix = norm_g[layer, 1]
        if layer == 0:
            qkv = _project(_da_proj_kernel, xs, mod, layer, g_mix, da_w_qkv[0].astype(BF16),
                           list(rope64), _rope_specs(), 3 * D, "da_proj")
            lam_init = 0.8 - 0.6 * math.exp(-0.3 * layer)
            lam = jnp.zeros((8, LANES), F32).at[:4, :HEAD].set(
                jnp.stack([da_lam_q1[0], da_lam_k1[0], da_lam_q2[0], da_lam_k2[0]]))
            const3 = lambda *_: (0, 0)
            o = _attention(functools.partial(_da_chains, lam_init=lam_init), qkv,
                           LANES, LANES, LANES, LANES, D, 2 * D, DA_HEADS,
                           extra=(lam, _row(da_subln_g[0])),
                           extra_specs=(pl.BlockSpec((8, LANES), const3), pl.BlockSpec((1, LANES), const3)),
                           name="da_attention")
            w_o = da_w_o[0]
        elif layer == 1:
            wq, wk, wv = jnp.split(gqa_w_qkv[0], [GQA_HEADS * HEAD, (GQA_HEADS + GQA_KV) * HEAD], axis=1)
            w = jnp.concatenate([wq, _dup_heads(wk, GQA_KV), _dup_heads(wv, GQA_KV)], axis=1).astype(BF16)
            gains = (_row(jnp.tile(gqa_q_norm_g[0], MXU_N // HEAD)), _row(jnp.tile(gqa_k_norm_g[0], MXU_N // HEAD)))
            qkv = _project(_gqa_proj_kernel, xs, mod, layer, g_mix, w,
                           list(rope64) + list(gains) + [ones_blk],
                           _rope_specs() + [_row_spec(MXU_N)] * 2 + [blk_spec],
                           D + 2 * GQA_KV * LANES, "gqa_proj")
            o = _attention(functools.partial(_pair_chains, wide_keys=False), qkv,
                           2 * LANES, LANES, LANES, 2 * LANES, D, D + GQA_KV * LANES, GQA_KV,
                           name="gqa_attention")
            w_o = gqa_w_o[0]
        elif layer == 2:
            wd = mla_w_down[0]
            n_lora = MLA_Q_LORA + MLA_KV_LORA
            wd = jnp.concatenate([wd[:, :n_lora], jnp.zeros((D, MLA_NOPE), F32), wd[:, n_lora:],
                                  jnp.zeros((D, LANES - MLA_NOPE - MLA_ROPE), F32)], axis=1).astype(BF16)
            wuq = _pad_heads(mla_w_uq[0], MLA_NOPE + MLA_ROPE).astype(BF16)
            wukv = mla_w_ukv[0].reshape(MLA_KV_LORA, MLA_HEADS, 2 * HEAD)
            wuk = _pad_heads(wukv[:, :, :MLA_NOPE].reshape(MLA_KV_LORA, -1), MLA_NOPE)
            wuv = wukv[:, :, MLA_NOPE:].reshape(MLA_KV_LORA, MLA_HEADS * HEAD)
            wukv = jnp.concatenate([wuk, wuv], axis=1).astype(BF16)
            n_q = MLA_HEADS * LANES
            const = lambda i: (0, 0)
            qkv = _project(_mla_proj_kernel, xs, mod, layer, g_mix, wd,
                           list(rope_mla) + [_row(mla_q_norm_g[0]), _row(mla_kv_norm_g[0]), wuq, wukv],
                           _rope_specs() + [_row_spec(MLA_Q_LORA), _row_spec(MLA_KV_LORA),
                                            _resident(wuq.shape, const), _resident(wukv.shape, const)],
                           2 * n_q + D, "mla_proj")
            o = _attention(functools.partial(_pair_chains, wide_keys=True), qkv,
                           2 * LANES, 2 * LANES, LANES, LANES, n_q, 2 * n_q, MLA_HEADS // 2,
                           name="mla_attention")
            w_o = mla_w_o[0]
        else:
            qkv = _project(_na_proj_kernel, xs, mod, layer, g_mix, na_w_qkv[0].astype(BF16), [], [], 3 * D, "na_proj")
            o = _na_attention(qkv, _na_bias(na_rpb[0]))
            w_o = na_w_o[0]
        xs = _ffn(xs, mod, layer, 1, norm_g[layer, 2], w_in, w_out, n_rows=N_LAT if last else N_TOK,
                  attn=(o, w_o.astype(BF16)), final_g=final_g if last else None)
    return xs.reshape(BATCH, SEQ, D)
```

```python
import collections
import functools
import math

import jax
import jax.numpy as jnp
import numpy as np
from jax import lax
from jax.experimental import pallas as pl
from jax.experimental.pallas import tpu as pltpu

F32 = jnp.float32
BF16 = jnp.bfloat16

D = 1024
BATCH = 8
SEQ = 2048
DEPTH = 4
GRID_W = 64
GRID_ROWS = SEQ // GRID_W
CTX = 256
N_MOD = 9
D_FF = 2816
ROPE_THETA = 10000.0
NEG_INF = -1e30
EPS = 1e-6
LOG2E = math.log2(math.e)

N_LAT = BATCH * SEQ
N_TOK = N_LAT + BATCH * CTX
N_GROUPS = N_TOK // SEQ
MOD_ROWS = 16
assert BATCH * CTX == SEQ and N_GROUPS == BATCH + 1 and N_GROUPS <= MOD_ROWS

LANES = 128
MXU_N = 256
HEAD = 64
FF_CHUNKS = D_FF // MXU_N
assert FF_CHUNKS * MXU_N == D_FF

DA_HEADS = 8
GQA_HEADS, GQA_KV = 16, 4
MLA_HEADS, MLA_Q_LORA, MLA_KV_LORA, MLA_NOPE, MLA_ROPE = 16, 256, 128, 64, 32
NA_HEADS, NA_WIN_ROWS, NA_WIN_COLS = 16, 8, 16
NA_TILE_ROWS = 4
NA_TILE_BAND = 12
NA_TILES = GRID_ROWS // NA_TILE_ROWS
NA_PAIRS_PER_STEP = 2
NA_VARIANT_TILES = (0, 1, NA_TILES - 1)
NA_TILE_VARIANT = tuple(0 if t == 0 else 2 if t == NA_TILES - 1 else 1 for t in range(NA_TILES))
assert NA_TILE_ROWS + NA_WIN_ROWS <= NA_TILE_BAND and (NA_TILE_BAND * GRID_W) % MXU_N == 0

TM = 512
CHAINS_PER_STEP = 16
SUB_Q = 256
CHAIN_N = 2 * SUB_Q
ATT_KC = 1024
MOD_TN = 2304
PROJ_TN = 512
LAT_TILES = SEQ // TM
assert SUB_Q == CTX == NA_TILE_ROWS * GRID_W
VMEM_LIMIT = 56 * 1024 * 1024

NT = (((1,), (1,)), ((), ()))
TN = (((0,), (0,)), ((), ()))


def _params(*sem):
    return pltpu.CompilerParams(dimension_semantics=sem, vmem_limit_bytes=VMEM_LIMIT)


def _resident(shape, index_map):
    return pl.BlockSpec(shape, index_map, pipeline_mode=pl.Buffered(1))


def _bdot(a, b):
    return jnp.dot(a, b, preferred_element_type=F32)


def _rms(x):
    return x * lax.rsqrt(jnp.mean(x * x, axis=-1, keepdims=True) + EPS)


def _modulated(x, g, shift, scale):
    return (_rms(x) * g) * (1.0 + scale) + shift


def _mod_kernel(c_ref, w_ref, b_ref, o_ref):
    c = c_ref[...]
    s = (c * jax.nn.sigmoid(c)).astype(BF16)
    o_ref[...] = _bdot(s, w_ref[...].astype(BF16)) + b_ref[...]


def _modulation(cvec, w_mod, b_mod):
    n = N_MOD * D
    return pl.pallas_call(
        _mod_kernel,
        grid=(DEPTH, n // MOD_TN),
        in_specs=[pl.BlockSpec((MOD_ROWS, D), lambda l, j: (0, 0)),
                  pl.BlockSpec((None, D, MOD_TN), lambda l, j: (l, 0, j)),
                  pl.BlockSpec((None, 1, MOD_TN), lambda l, j: (l, 0, j))],
        out_specs=pl.BlockSpec((None, MOD_ROWS, MOD_TN), lambda l, j: (l, 0, j)),
        out_shape=jax.ShapeDtypeStruct((DEPTH, MOD_ROWS, n), F32),
        compiler_params=_params("arbitrary", "arbitrary"),
        name="modulation",
    )(cvec, w_mod, b_mod.reshape(DEPTH, 1, n))


def _row_specs(src):
    if not isinstance(src, tuple):
        return [pl.BlockSpec((TM, src.shape[1]), lambda i: (i, 0))], [src]
    lat, ctx = src
    n_lat = lat.shape[0] // TM
    return ([pl.BlockSpec((TM, lat.shape[1]), lambda i: (jnp.minimum(i, n_lat - 1), 0)),
             pl.BlockSpec((TM, ctx.shape[1]), lambda i: (jnp.maximum(i - n_lat, 0), 0))], [lat, ctx])


def _read_rows(refs):
    if len(refs) == 1:
        return refs[0][...]
    return jnp.where(pl.program_id(0) < N_LAT // TM, refs[0][...], refs[1][...])


def _ffn_kernel(*refs, n_x, n_o, mod_row, final_norm):
    it = iter(refs)
    x_refs = [next(it) for _ in range(n_x)]
    mod_ref, g_ref, win_ref, wout_ref = (next(it) for _ in range(4))
    o_refs = [next(it) for _ in range(n_o)]
    wo_ref = next(it) if n_o else None
    fg_ref = next(it) if final_norm else None
    out_ref, a_scr = next(it), next(it)

    x = _read_rows(x_refs)
    mod = mod_ref[...]
    if n_o:
        x = x + mod[5:6] * _bdot(_read_rows(o_refs), wo_ref[...])
    shift, scale, gate = (mod[mod_row + k:mod_row + k + 1] for k in range(3))
    h = _modulated(x, g_ref[...], shift, scale).astype(BF16)
    for c in range(FF_CHUNKS):
        lo = c * MXU_N
        gt = _bdot(h, win_ref[:, lo:lo + MXU_N])
        up = _bdot(h, win_ref[:, D_FF + lo:D_FF + lo + MXU_N])
        a_scr[:, lo:lo + MXU_N] = ((gt * jax.nn.sigmoid(gt)) * up).astype(BF16)
    y = x + (0.5 * gate) * _bdot(a_scr[...], wout_ref[...])
    if final_norm:
        y = _rms(y) * fg_ref[...]
    out_ref[...] = y


def _ffn(x, mod, layer, which, g, w_in, w_out, *, n_rows, attn=None, final_g=None):
    const = lambda i: (0, 0)
    x_specs, x_args = _row_specs(x)
    in_specs = x_specs + [pl.BlockSpec((None, None, N_MOD, D), lambda i: (layer, i // LAT_TILES, 0, 0)),
                          _resident((1, D), const),
                          _resident((None, None, D, 2 * D_FF), lambda i: (layer, which, 0, 0)),
                          _resident((None, None, D_FF, D), lambda i: (layer, which, 0, 0))]
    args = x_args + [mod, g.reshape(1, D), w_in, w_out]
    n_o = 0
    if attn is not None:
        o, w_o = attn
        o_specs, o_args = _row_specs(o)
        n_o = len(o_args)
        in_specs += o_specs + [_resident((D, D), const)]
        args += o_args + [w_o]
    if final_g is not None:
        in_specs.append(_resident((1, D), const))
        args.append(final_g.reshape(1, D))
    return pl.pallas_call(
        functools.partial(_ffn_kernel, n_x=len(x_args), n_o=n_o, mod_row=6 * which,
                          final_norm=final_g is not None),
        grid=(n_rows // TM,),
        in_specs=in_specs,
        out_specs=pl.BlockSpec((TM, D), lambda i: (i, 0)),
        out_shape=jax.ShapeDtypeStruct((n_rows, D), F32),
        scratch_shapes=[pltpu.VMEM((TM, D_FF), BF16)],
        compiler_params=_params("arbitrary"),
        name="ffn",
    )(*args)


def _rope_tables(rd, lane0, period):
    t = jnp.arange(SEQ)
    half, nf = rd // 2, rd // 4
    inv_freq = ROPE_THETA ** (-jnp.arange(nf, dtype=F32) / nf)
    ang_r = (t // GRID_W).astype(F32)[:, None] * inv_freq
    ang_c = (t % GRID_W).astype(F32)[:, None] * inv_freq
    zero = jnp.zeros((SEQ, nf), F32)
    cos = jnp.concatenate([jnp.cos(ang_r)] * 2 + [jnp.cos(ang_c)] * 2, axis=1)
    s_up = jnp.concatenate([-jnp.sin(ang_r), zero, -jnp.sin(ang_c), zero], axis=1)
    s_dn = jnp.concatenate([zero, jnp.sin(ang_r), zero, jnp.sin(ang_c)], axis=1)
    assert half == 2 * nf and LANES % period == 0 and lane0 + rd <= period

    def place(blk, fill):
        pat = jnp.full((SEQ, period), fill, F32).at[:, lane0:lane0 + rd].set(blk)
        pat = jnp.tile(pat, (1, LANES // period))
        return jnp.concatenate([pat, jnp.full((TM, LANES), fill, F32)], axis=0)

    return place(cos, 1.0), place(s_up, 0.0), place(s_dn, 0.0)


def _rope(x, cos, s_up, s_dn, q):
    return x * cos + pltpu.roll(x, LANES - q, 1) * s_up + pltpu.roll(x, q, 1) * s_dn


def _rope_specs():
    m = lambda i: (jnp.where(i < BATCH * LAT_TILES, i % LAT_TILES, LAT_TILES), 0)
    return [pl.BlockSpec((TM, LANES), m)] * 3


def _blocks(n_cols):
    return range(0, n_cols, LANES)


def _head_sumsq(x, ones_blk):
    xx = x * x
    hi = xx.astype(BF16)
    lo = (xx - hi.astype(F32)).astype(BF16)
    return _bdot(hi, ones_blk) + _bdot(lo, ones_blk)


def _mixer_input(x_ref, mod_ref, g_ref):
    mod = mod_ref[...]
    return _modulated(x_ref[...], g_ref[...], mod[3:4], mod[4:5]).astype(BF16)


def _qk_tables(c_ref, su_ref, sd_ref):
    k_tabs = (c_ref[...], su_ref[...], sd_ref[...])
    return tuple(t * (HEAD ** -0.5 * LOG2E) for t in k_tabs), k_tabs


def _da_proj_kernel(x_ref, mod_ref, g_ref, w_ref, c_ref, su_ref, sd_ref, out_ref):
    h = _mixer_input(x_ref, mod_ref, g_ref)
    q_tabs, k_tabs = _qk_tables(c_ref, su_ref, sd_ref)
    for c0 in range(0, 2 * D, PROJ_TN):
        y = _bdot(h, w_ref[:, c0:c0 + PROJ_TN])
        for lo in _blocks(PROJ_TN):
            tabs = q_tabs if c0 < D else k_tabs
            out_ref[:, c0 + lo:c0 + lo + LANES] = _rope(y[:, lo:lo + LANES], *tabs, HEAD // 4).astype(BF16)
    out_ref[:, 2 * D:] = _bdot(h, w_ref[:, 2 * D:]).astype(BF16)


def _gqa_proj_kernel(x_ref, mod_ref, g_ref, w_ref, c_ref, su_ref, sd_ref, qg_ref, kg_ref, ones_ref, out_ref):
    h = _mixer_input(x_ref, mod_ref, g_ref)
    q_tabs, k_tabs = _qk_tables(c_ref, su_ref, sd_ref)
    ones_blk = ones_ref[...]
    n_qk = D + GQA_KV * LANES
    for c0 in range(0, n_qk, PROJ_TN):
        y = _bdot(h, w_ref[:, c0:c0 + PROJ_TN])
        tabs, gain = (q_tabs, qg_ref[...]) if c0 < D else (k_tabs, kg_ref[...])
        for m0 in range(0, PROJ_TN, MXU_N):
            ym = y[:, m0:m0 + MXU_N]
            ym = (ym * lax.rsqrt(_head_sumsq(ym, ones_blk) * (1.0 / HEAD) + EPS)) * gain
            for lo in _blocks(MXU_N):
                out_ref[:, c0 + m0 + lo:c0 + m0 + lo + LANES] = _rope(
                    ym[:, lo:lo + LANES], *tabs, HEAD // 4).astype(BF16)
    out_ref[:, n_qk:] = _bdot(h, w_ref[:, n_qk:]).astype(BF16)


def _mla_proj_kernel(x_ref, mod_ref, g_ref, wd_ref, c_ref, su_ref, sd_ref, qg_ref, kvg_ref,
                     wuq_ref, wukv_ref, out_ref):
    h = _mixer_input(x_ref, mod_ref, g_ref)
    c, su, sd = c_ref[...], su_ref[...], sd_ref[...]
    d = _bdot(h, wd_ref[...])
    cq = (_rms(d[:, :MLA_Q_LORA]) * qg_ref[...]).astype(BF16)
    ckv = (_rms(d[:, MLA_Q_LORA:MLA_Q_LORA + MLA_KV_LORA]) * kvg_ref[...]).astype(BF16)
    k_pe = _rope(d[:, MLA_Q_LORA + MLA_KV_LORA:], c, su, sd, MLA_ROPE // 4)
    n_q = MLA_HEADS * LANES
    for c0 in range(0, n_q, PROJ_TN):
        q = _bdot(cq, wuq_ref[:, c0:c0 + PROJ_TN])
        k = _bdot(ckv, wukv_ref[:, c0:c0 + PROJ_TN])
        for lo in _blocks(PROJ_TN):
            qh = _rope(q[:, lo:lo + LANES], c, su, sd, MLA_ROPE // 4) * ((MLA_NOPE + MLA_ROPE) ** -0.5 * LOG2E)
            out_ref[:, c0 + lo:c0 + lo + LANES] = qh.astype(BF16)
            out_ref[:, n_q + c0 + lo:n_q + c0 + lo + LANES] = (k[:, lo:lo + LANES] + k_pe).astype(BF16)
    out_ref[:, 2 * n_q:] = _bdot(ckv, wukv_ref[:, n_q:]).astype(BF16)


def _na_proj_kernel(x_ref, mod_ref, g_ref, w_ref, out_ref):
    h = _mixer_input(x_ref, mod_ref, g_ref)
    out_ref[:, :D] = (_bdot(h, w_ref[:, :D]) * (HEAD ** -0.5 * LOG2E)).astype(BF16)
    out_ref[:, D:] = _bdot(h, w_ref[:, D:]).astype(BF16)


def _project(kernel, x, mod, layer, g, w, extra, extra_specs, out_cols, name):
    const = lambda i: (0, 0)
    in_specs = [pl.BlockSpec((TM, D), lambda i: (i, 0)),
                pl.BlockSpec((None, None, N_MOD, D), lambda i: (layer, i // LAT_TILES, 0, 0)),
                _resident((1, D), const),
                _resident(w.shape, const)] + extra_specs
    return pl.pallas_call(
        kernel,
        grid=(N_TOK // TM,),
        in_specs=in_specs,
        out_specs=pl.BlockSpec((TM, out_cols), lambda i: (i, 0)),
        out_shape=jax.ShapeDtypeStruct((N_TOK, out_cols), BF16),
        compiler_params=_params("arbitrary"),
        name=name,
    )(x, mod, g.reshape(1, D), w, *extra)


def _row(v):
    return v.reshape(1, -1).astype(F32)


def _row_spec(n):
    return _resident((1, n), lambda i: (0, 0))


def _stack_halves(q):
    lane = lax.broadcasted_iota(jnp.int32, (1, LANES), 1)
    m0 = (lane < HEAD).astype(F32)
    qf = q.astype(F32)
    return jnp.concatenate([(qf * m0).astype(BF16), (qf * (1.0 - m0)).astype(BF16)], axis=0)


def _stack_wide(q):
    qa, qb = q[:, :LANES], q[:, LANES:]
    zero = jnp.zeros_like(qa)
    return jnp.concatenate([jnp.concatenate([qa, zero], axis=1), jnp.concatenate([zero, qb], axis=1)], axis=0)


def _pair_out(o_t):
    rows = o_t.shape[1] // 2
    return jnp.concatenate([o_t[:HEAD, :rows], o_t[HEAD:, rows:]], axis=0).T


_Chain = collections.namedtuple("_Chain", "qs segs bias emit")


def _run_chains(chains, s_scr):
    chunks = [[(k, v, lo + c, min(ATT_KC, n - c)) for k, v, lo, n in ch.segs for c in range(0, n, ATT_KC)]
              for ch in chains]
    offs = [[sum(n for _, _, _, n in cl[:j]) for j in range(len(cl))] for cl in chunks]
    stats = [dict(m=None, l=None, acc=None) for _ in chains]

    def score(i, j):
        ch, st = chains[i], stats[i]
        k_ref, _, lo, n = chunks[i][j]
        s = lax.dot_general(k_ref[lo:lo + n, :], ch.qs, NT, preferred_element_type=F32)
        b = ch.bias(offs[i][j], n) if ch.bias is not None else None
        if b is not None:
            s = s + b
        s_scr[i % 2, offs[i][j]:offs[i][j] + n, :] = s
        mj = jnp.max(s, axis=0, keepdims=True)
        st["m"] = mj if st["m"] is None else jnp.maximum(st["m"], mj)

    def value(i, j):
        st = stats[i]
        _, v_ref, lo, n = chunks[i][j]
        e = jnp.exp2(s_scr[i % 2, offs[i][j]:offs[i][j] + n, :] - st["m"])
        lj = jnp.sum(e, axis=0, keepdims=True)
        pv = lax.dot_general(v_ref[lo:lo + n, :], e.astype(BF16), TN, preferred_element_type=F32)
        st["l"] = lj if st["l"] is None else st["l"] + lj
        st["acc"] = pv if st["acc"] is None else st["acc"] + pv

    for i in range(len(chains) + 1):
        n_score = len(chunks[i]) if i < len(chains) else 0
        n_value = len(chunks[i - 1]) if i > 0 else 0
        for j in range(max(n_score, n_value)):
            if j < n_score:
                score(i, j)
            if j < n_value:
                value(i - 1, j)
        if i > 0:
            chains[i - 1].emit(stats[i - 1]["acc"] / stats[i - 1]["l"])


def _attn_refs(refs, n_seg, n_extra):
    kv = refs[1:1 + 2 * n_seg]
    segs = [(kv[2 * s], kv[2 * s + 1], 0, kv[2 * s].shape[0]) for s in range(n_seg)]
    return refs[0], segs, refs[1 + 2 * n_seg:1 + 2 * n_seg + n_extra], refs[-2], refs[-1]


def _attn_kernel(*refs, build, n_seg, n_extra, head_cols):
    q_ref, segs, extras, o_ref, s_scr = _attn_refs(refs, n_seg, n_extra)
    if head_cols is None:
        chains = build(q_ref, segs, extras, o_ref)
    else:
        qc, kc, vc, oc = head_cols
        chains = []
        for hb in range(q_ref.shape[1] // qc):
            view = lambda ref, w: ref.at[:, hb * w:(hb + 1) * w]
            segs_h = [(view(k, kc), view(v, vc), lo, n) for k, v, lo, n in segs]
            chains += build(view(q_ref, qc), segs_h, extras, view(o_ref, oc))
    _run_chains(chains, s_scr)


def _da_chains(q_ref, segs, extras, o_ref, *, lam_init):
    lam_ref, g_ref = extras
    lv = lam_ref[...]
    lam = (jnp.exp(jnp.sum(lv[0:1] * lv[1:2], axis=-1, keepdims=True))
           - jnp.exp(jnp.sum(lv[2:3] * lv[3:4], axis=-1, keepdims=True)) + lam_init)

    def emit(rows):
        def f(o_t):
            o = o_t[:, :SUB_Q] - lam * o_t[:, SUB_Q:]
            o = o * lax.rsqrt(jnp.mean(o * o, axis=0, keepdims=True) + EPS)
            o_ref[rows, :] = ((o.T * g_ref[...]) * (1.0 - lam_init)).astype(BF16)
        return f

    chains = []
    for lo in range(0, q_ref.shape[0], SUB_Q):
        rows = slice(lo, lo + SUB_Q)
        chains.append(_Chain(_stack_halves(q_ref[rows, :]), segs, None, emit(rows)))
    return chains


def _pair_chains(q_ref, segs, extras, o_ref, *, wide_keys):
    def emit(rows, cols):
        def f(o_t):
            o_ref[rows, cols] = _pair_out(o_t).astype(BF16)
        return f

    chains = []
    for lo in range(0, q_ref.shape[0], SUB_Q):
        rows = slice(lo, lo + SUB_Q)
        if wide_keys:
            chains.append(_Chain(_stack_wide(q_ref[rows, :]), segs, None, emit(rows, slice(0, LANES))))
        else:
            for c in _blocks(q_ref.shape[1]):
                cols = slice(c, c + LANES)
                chains.append(_Chain(_stack_halves(q_ref[rows, cols]), segs, None, emit(rows, cols)))
    return chains


def _attention(build, qkv, q_cols, k_cols, v_cols, o_cols, k0, v0, n_hb, extra=(), extra_specs=(), name=""):
    kb, vb = k0 // k_cols, v0 // v_cols
    ctx_blk = N_LAT // CTX
    scratch = lambda n_keys: [pltpu.VMEM((2, n_keys, CHAIN_N), F32)]
    chains_per_head = (SEQ // SUB_Q) * (o_cols // LANES)
    tq = SEQ * min(CHAINS_PER_STEP, chains_per_head) // chains_per_head
    hps = max(1, CHAINS_PER_STEP // chains_per_head)
    n_qt = SEQ // tq
    assert n_hb % hps == 0 and kb % hps == 0 and vb % hps == 0
    kernel = functools.partial(_attn_kernel, build=build, n_extra=len(extra))
    head_cols = (q_cols, k_cols, v_cols, o_cols)
    lat = pl.pallas_call(
        functools.partial(kernel, n_seg=2, head_cols=head_cols if hps > 1 else None),
        grid=(BATCH, n_hb // hps, n_qt),
        in_specs=[pl.BlockSpec((tq, hps * q_cols), lambda b, h, t: (b * n_qt + t, h)),
                  pl.BlockSpec((SEQ, hps * k_cols), lambda b, h, t: (b, kb // hps + h)),
                  pl.BlockSpec((SEQ, hps * v_cols), lambda b, h, t: (b, vb // hps + h)),
                  pl.BlockSpec((CTX, hps * k_cols), lambda b, h, t: (ctx_blk + b, kb // hps + h)),
                  pl.BlockSpec((CTX, hps * v_cols), lambda b, h, t: (ctx_blk + b, vb // hps + h))]
                 + list(extra_specs),
        out_specs=pl.BlockSpec((tq, hps * o_cols), lambda b, h, t: (b * n_qt + t, h)),
        out_shape=jax.ShapeDtypeStruct((N_LAT, D), BF16),
        scratch_shapes=scratch(SEQ + CTX),
        compiler_params=_params("arbitrary", "arbitrary", "arbitrary"),
        name=name,
    )(qkv, qkv, qkv, qkv, qkv, *extra)
    assert n_hb * o_cols == D and k0 % (n_hb * k_cols) == 0 and v0 % (n_hb * v_cols) == 0
    ctx = pl.pallas_call(
        functools.partial(kernel, n_seg=1, head_cols=head_cols),
        grid=(BATCH,),
        in_specs=[pl.BlockSpec((CTX, n_hb * q_cols), lambda b: (ctx_blk + b, 0)),
                  pl.BlockSpec((CTX, n_hb * k_cols), lambda b: (ctx_blk + b, kb // n_hb)),
                  pl.BlockSpec((CTX, n_hb * v_cols), lambda b: (ctx_blk + b, vb // n_hb))] + list(extra_specs),
        out_specs=pl.BlockSpec((CTX, D), lambda b: (b, 0)),
        out_shape=jax.ShapeDtypeStruct((BATCH * CTX, D), BF16),
        scratch_shapes=scratch(CTX),
        compiler_params=_params("arbitrary"),
        name=name + "_ctx",
    )(qkv, qkv, qkv, *extra)
    return lat, ctx


def _na_tile_band(t):
    r0 = t * NA_TILE_ROWS
    return r0, min(max(r0 - NA_WIN_ROWS // 2, 0), GRID_ROWS - NA_TILE_BAND)


def _na_attn_kernel(q_ref, kl_ref, vl_ref, kc_ref, vc_ref, bias_ref, o_ref, s_scr):
    n_band = NA_TILE_BAND * GRID_W
    _, plan = _na_bias_plan()

    def emit(rows, cols):
        def f(o_t):
            o_ref[rows, cols] = _pair_out(o_t).astype(BF16)
        return f

    def band_bias(hp, variant):
        def f(off, n):
            if off + n > n_band:
                return None
            key_rows = [jnp.concatenate([bias_ref[hp, j, plan[variant, kr, p]] for j in range(2)
                                         for p in range(NA_TILE_ROWS // 2)], axis=1)
                        for kr in range(off // GRID_W, (off + n) // GRID_W)]
            return jnp.concatenate(key_rows, axis=0)
        return f

    chains = []
    for hp in range(NA_PAIRS_PER_STEP):
        cols = slice(hp * LANES, (hp + 1) * LANES)
        view = lambda ref: ref.at[:, cols]
        for t in range(NA_TILES):
            r0, b0 = _na_tile_band(t)
            rows = slice(r0 * GRID_W, (r0 + NA_TILE_ROWS) * GRID_W)
            segs = [(view(kl_ref), view(vl_ref), b0 * GRID_W, n_band), (view(kc_ref), view(vc_ref), 0, CTX)]
            chains.append(_Chain(_stack_halves(q_ref[rows, cols]), segs,
                                 band_bias(hp, NA_TILE_VARIANT[t]), emit(rows, cols)))
    _run_chains(chains, s_scr)


def _toeplitz(v):
    lead, w, period = v.shape[:-1], NA_WIN_COLS, 2 * GRID_W - 1
    line = jnp.zeros(lead + (period,), v.dtype)
    line = line.at[..., :w].set(v[..., w - 1:]).at[..., period - (w - 1):].set(v[..., :w - 1])
    x = jnp.broadcast_to(line[..., None, :], lead + (GRID_W, period)).reshape(lead + (GRID_W * period,))
    skew = x[..., :GRID_W * (period - 1)].reshape(lead + (GRID_W, period - 1))
    return jnp.swapaxes(skew[..., :GRID_W], -1, -2)


def _na_bias_plan():
    pairs, plan = [], {}
    for variant, t in enumerate(NA_VARIANT_TILES):
        r0, b0 = _na_tile_band(t)
        for kr in range(NA_TILE_BAND):
            for p in range(NA_TILE_ROWS // 2):
                pair = []
                for r in (r0 + 2 * p, r0 + 2 * p + 1):
                    rs = min(max(r - NA_WIN_ROWS // 2, 0), GRID_ROWS - NA_WIN_ROWS)
                    pair.append(b0 + kr - r + NA_WIN_ROWS - 1 if rs <= b0 + kr < rs + NA_WIN_ROWS else None)
                if tuple(pair) not in pairs:
                    pairs.append(tuple(pair))
                plan[variant, kr, p] = pairs.index(tuple(pair))
    return pairs, plan


def _na_bias(rpb):
    cols = np.arange(GRID_W)
    col_start = np.clip(cols - NA_WIN_COLS // 2, 0, GRID_W - NA_WIN_COLS)
    col_in = (cols[:, None] >= col_start[None, :]) & (cols[:, None] < col_start[None, :] + NA_WIN_COLS)
    blocks = jnp.where(col_in, _toeplitz(rpb) * LOG2E, NEG_INF)
    outside = jnp.full((NA_HEADS, GRID_W, GRID_W), NEG_INF, F32)
    half = lambda x: outside if x is None else blocks[:, x]
    pairs, _ = _na_bias_plan()
    tab = jnp.stack([jnp.concatenate([half(x0), half(x1)], axis=-1) for x0, x1 in pairs], axis=1)
    return tab.reshape(NA_HEADS // 2, 2, len(pairs), GRID_W, LANES).astype(F32)


def _na_attention(qkv, bias):
    w = NA_PAIRS_PER_STEP * LANES
    n_hb = D // w
    kb, vb = D // w, 2 * D // w
    in_specs = [pl.BlockSpec((SEQ, w), lambda h, b: (b, h)),
                pl.BlockSpec((SEQ, w), lambda h, b: (b, kb + h)),
                pl.BlockSpec((SEQ, w), lambda h, b: (b, vb + h)),
                pl.BlockSpec((CTX, w), lambda h, b: (N_LAT // CTX + b, kb + h)),
                pl.BlockSpec((CTX, w), lambda h, b: (N_LAT // CTX + b, vb + h)),
                pl.BlockSpec((NA_PAIRS_PER_STEP,) + bias.shape[1:], lambda h, b: (h, 0, 0, 0, 0))]
    return pl.pallas_call(
        _na_attn_kernel,
        grid=(n_hb, BATCH),
        in_specs=in_specs,
        out_specs=pl.BlockSpec((SEQ, w), lambda h, b: (b, h)),
        out_shape=jax.ShapeDtypeStruct((N_LAT, D), BF16),
        scratch_shapes=[pltpu.VMEM((2, NA_TILE_BAND * GRID_W + CTX, CHAIN_N), F32)],
        compiler_params=_params("arbitrary", "arbitrary"),
        name="na_attention",
    )(qkv, qkv, qkv, qkv, qkv, bias)


def _dup_heads(w, n_heads):
    w = w.reshape(D, n_heads, 1, HEAD)
    return jnp.broadcast_to(w, (D, n_heads, LANES // HEAD, HEAD)).reshape(D, n_heads * LANES)


def _pad_heads(w, width):
    k = w.shape[0]
    w = w.reshape(k, MLA_HEADS, width)
    return jnp.pad(w, ((0, 0), (0, 0), (0, LANES - width))).reshape(k, MLA_HEADS * LANES)


def kernel(x, c, ctx, c_ctx, w_mod, b_mod, norm_g, w_ffn_in, w_ffn_out, da_w_qkv, da_lam_q1, da_lam_k1, da_lam_q2, da_lam_k2, da_subln_g, da_w_o, gqa_w_qkv, gqa_q_norm_g, gqa_k_norm_g, gqa_w_o, mla_w_down, mla_q_norm_g, mla_kv_norm_g, mla_w_uq, mla_w_ukv, mla_w_o, na_w_qkv, na_rpb, na_w_o, final_g):
    assert x.shape == (BATCH, SEQ, D) and ctx.shape == (BATCH, CTX, D) and w_mod.shape[0] == DEPTH == 4

    cvec = jnp.zeros((MOD_ROWS, D), F32).at[:BATCH].set(c).at[BATCH].set(c_ctx)
    mod = _modulation(cvec, w_mod, b_mod).reshape(DEPTH, MOD_ROWS, N_MOD, D)
    xs = (x.reshape(N_LAT, D), ctx.reshape(BATCH * CTX, D))
    w_in, w_out = w_ffn_in.astype(BF16), w_ffn_out.astype(BF16)

    rope64 = _rope_tables(HEAD, 0, HEAD)
    rope_mla = _rope_tables(MLA_ROPE, MLA_NOPE, LANES)
    ones_blk = jnp.kron(jnp.eye(MXU_N // HEAD, dtype=F32), jnp.ones((HEAD, HEAD), F32)).astype(BF16)
    blk_spec = _resident((MXU_N, MXU_N), lambda i: (0, 0))

    for layer in range(DEPTH):
        last = layer == DEPTH - 1
        xs = _ffn(xs, mod, layer, 0, norm_g[layer, 0], w_in, w_out, n_rows=N_TOK)
        g_mix = norm_g[layer, 1]
        if layer == 0:
            qkv = _project(_da_proj_kernel, xs, mod, layer, g_mix, da_w_qkv[0].astype(BF16),
                           list(rope64), _rope_specs(), 3 * D, "da_proj")
            lam_init = 0.8 - 0.6 * math.exp(-0.3 * layer)
            lam = jnp.zeros((8, LANES), F32).at[:4, :HEAD].set(
                jnp.stack([da_lam_q1[0], da_lam_k1[0], da_lam_q2[0], da_lam_k2[0]]))
            const3 = lambda *_: (0, 0)
            o = _attention(functools.partial(_da_chains, lam_init=lam_init), qkv,
                           LANES, LANES, LANES, LANES, D, 2 * D, DA_HEADS,
                           extra=(lam, _row(da_subln_g[0])),
                           extra_specs=(pl.BlockSpec((8, LANES), const3), pl.BlockSpec((1, LANES), const3)),
                           name="da_attention")
            w_o = da_w_o[0]
        elif layer == 1:
            wq, wk, wv = jnp.split(gqa_w_qkv[0], [GQA_HEADS * HEAD, (GQA_HEADS + GQA_KV) * HEAD], axis=1)
            w = jnp.concatenate([wq, _dup_heads(wk, GQA_KV), _dup_heads(wv, GQA_KV)], axis=1).astype(BF16)
            gains = (_row(jnp.tile(gqa_q_norm_g[0], MXU_N // HEAD)), _row(jnp.tile(gqa_k_norm_g[0], MXU_N // HEAD)))
            qkv = _project(_gqa_proj_kernel, xs, mod, layer, g_mix, w,
                           list(rope64) + list(gains) + [ones_blk],
                           _rope_specs() + [_row_spec(MXU_N)] * 2 + [blk_spec],
                           D + 2 * GQA_KV * LANES, "gqa_proj")
            o = _attention(functools.partial(_pair_chains, wide_keys=False), qkv,
                           2 * LANES, LANES, LANES, 2 * LANES, D, D + GQA_KV * LANES, GQA_KV,
                           name="gqa_attention")
            w_o = gqa_w_o[0]
        elif layer == 2:
            wd = mla_w_down[0]
            n_lora = MLA_Q_LORA + MLA_KV_LORA
            wd = jnp.concatenate([wd[:, :n_lora], jnp.zeros((D, MLA_NOPE), F32), wd[:, n_lora:],
                                  jnp.zeros((D, LANES - MLA_NOPE - MLA_ROPE), F32)], axis=1).astype(BF16)
            wuq = _pad_heads(mla_w_uq[0], MLA_NOPE + MLA_ROPE).astype(BF16)
            wukv = mla_w_ukv[0].reshape(MLA_KV_LORA, MLA_HEADS, 2 * HEAD)
            wuk = _pad_heads(wukv[:, :, :MLA_NOPE].reshape(MLA_KV_LORA, -1), MLA_NOPE)
            wuv = wukv[:, :, MLA_NOPE:].reshape(MLA_KV_LORA, MLA_HEADS * HEAD)
            wukv = jnp.concatenate([wuk, wuv], axis=1).astype(BF16)
            n_q = MLA_HEADS * LANES
            const = lambda i: (0, 0)
            qkv = _project(_mla_proj_kernel, xs, mod, layer, g_mix, wd,
                           list(rope_mla) + [_row(mla_q_norm_g[0]), _row(mla_kv_norm_g[0]), wuq, wukv],
                           _rope_specs() + [_row_spec(MLA_Q_LORA), _row_spec(MLA_KV_LORA),
                                            _resident(wuq.shape, const), _resident(wukv.shape, const)],
                           2 * n_q + D, "mla_proj")
            o = _attention(functools.partial(_pair_chains, wide_keys=True), qkv,
                           2 * LANES, 2 * LANES, LANES, LANES, n_q, 2 * n_q, MLA_HEADS // 2,
                           name="mla_attention")
            w_o = mla_w_o[0]
        else:
            qkv = _project(_na_proj_kernel, xs, mod, layer, g_mix, na_w_qkv[0].astype(BF16), [], [], 3 * D, "na_proj")
            o = _na_attention(qkv, _na_bias(na_rpb[0]))
            w_o = na_w_o[0]
        xs = _ffn(xs, mod, layer, 1, norm_g[layer, 2], w_in, w_out, n_rows=N_LAT if last else N_TOK,
                  attn=(o, w_o.astype(BF16)), final_g=final_g if last else None)
    return xs.reshape(BATCH, SEQ, D)
```

```python
import collections
import functools
import math

import jax
import jax.numpy as jnp
import numpy as np
from jax import lax
from jax.experimental import pallas as pl
from jax.experimental.pallas import tpu as pltpu

F32 = jnp.float32
BF16 = jnp.bfloat16

D = 1024
BATCH = 8
SEQ = 2048
DEPTH = 4
GRID_W = 64
GRID_ROWS = SEQ // GRID_W
CTX = 256
N_MOD = 9
D_FF = 2816
ROPE_THETA = 10000.0
NEG_INF = -1e30
EPS = 1e-6
LOG2E = math.log2(math.e)

N_LAT = BATCH * SEQ
N_TOK = N_LAT + BATCH * CTX
N_GROUPS = N_TOK // SEQ
MOD_ROWS = 16
assert BATCH * CTX == SEQ and N_GROUPS == BATCH + 1 and N_GROUPS <= MOD_ROWS

LANES = 128
MXU_N = 256
HEAD = 64
FF_CHUNKS = D_FF // MXU_N
assert FF_CHUNKS * MXU_N == D_FF

DA_HEADS = 8
GQA_HEADS, GQA_KV = 16, 4
MLA_HEADS, MLA_Q_LORA, MLA_KV_LORA, MLA_NOPE, MLA_ROPE = 16, 256, 128, 64, 32
NA_HEADS, NA_WIN_ROWS, NA_WIN_COLS = 16, 8, 16
NA_TILE_ROWS = 4
NA_TILE_BAND = 12
NA_TILES = GRID_ROWS // NA_TILE_ROWS
NA_PAIRS_PER_STEP = 2
NA_VARIANT_TILES = (0, 1, NA_TILES - 1)
NA_TILE_VARIANT = tuple(0 if t == 0 else 2 if t == NA_TILES - 1 else 1 for t in range(NA_TILES))
assert NA_TILE_ROWS + NA_WIN_ROWS <= NA_TILE_BAND and (NA_TILE_BAND * GRID_W) % MXU_N == 0

TM = 512
CHAINS_PER_STEP = 16
SUB_Q = 256
CHAIN_N = 2 * SUB_Q
ATT_KC = 1024
MOD_TN = 2304
PROJ_TN = 512
CAST_IN_ROWS, CAST_OUT_ROWS = 32, 176
LAT_TILES = SEQ // TM
assert SUB_Q == CTX == NA_TILE_ROWS * GRID_W
VMEM_LIMIT = 56 * 1024 * 1024

NT = (((1,), (1,)), ((), ()))
TN = (((0,), (0,)), ((), ()))


def _params(*sem):
    return pltpu.CompilerParams(dimension_semantics=sem, vmem_limit_bytes=VMEM_LIMIT)


def _resident(shape, index_map):
    return pl.BlockSpec(shape, index_map, pipeline_mode=pl.Buffered(1))


def _bdot(a, b):
    return jnp.dot(a, b, preferred_element_type=F32)


def _rms(x):
    return x * lax.rsqrt(jnp.mean(x * x, axis=-1, keepdims=True) + EPS)


def _modulated(x, g, shift, scale):
    return (_rms(x) * g) * (1.0 + scale) + shift


def _mod_kernel(c_ref, w_ref, b_ref, o_ref):
    c = c_ref[...]
    s = (c * jax.nn.sigmoid(c)).astype(BF16)
    o_ref[...] = _bdot(s, w_ref[...].astype(BF16)) + b_ref[...]


def _modulation(cvec, w_mod, b_mod):
    n = N_MOD * D
    return pl.pallas_call(
        _mod_kernel,
        grid=(DEPTH, n // MOD_TN),
        in_specs=[pl.BlockSpec((MOD_ROWS, D), lambda l, j: (0, 0)),
                  pl.BlockSpec((None, D, MOD_TN), lambda l, j: (l, 0, j)),
                  pl.BlockSpec((None, 1, MOD_TN), lambda l, j: (l, 0, j))],
        out_specs=pl.BlockSpec((None, MOD_ROWS, MOD_TN), lambda l, j: (l, 0, j)),
        out_shape=jax.ShapeDtypeStruct((DEPTH, MOD_ROWS, n), F32),
        compiler_params=_params("arbitrary", "arbitrary"),
        name="modulation",
    )(cvec, w_mod, b_mod.reshape(DEPTH, 1, n))


def _row_specs(src):
    if not isinstance(src, tuple):
        return [pl.BlockSpec((TM, src.shape[1]), lambda i: (i, 0))], [src]
    lat, ctx = src
    n_lat = lat.shape[0] // TM
    return ([pl.BlockSpec((TM, lat.shape[1]), lambda i: (jnp.minimum(i, n_lat - 1), 0)),
             pl.BlockSpec((TM, ctx.shape[1]), lambda i: (jnp.maximum(i - n_lat, 0), 0))], [lat, ctx])


def _read_rows(refs):
    if len(refs) == 1:
        return refs[0][...]
    return jnp.where(pl.program_id(0) < N_LAT // TM, refs[0][...], refs[1][...])


def _ffn_kernel(*refs, n_x, n_o, mod_row, final_norm, cast_next):
    it = iter(refs)
    x_refs = [next(it) for _ in range(n_x)]
    mod_ref, g_ref, win_ref, wout_ref = (next(it) for _ in range(4))
    o_refs = [next(it) for _ in range(n_o)]
    wo_ref = next(it) if n_o else None
    fg_ref = next(it) if final_norm else None
    cast_in = [next(it) for _ in range(2 * cast_next)]
    out_ref = next(it)
    cast_out = [next(it) for _ in range(2 * cast_next)]
    a_scr = next(it)
    for src, dst in zip(cast_in, cast_out):
        dst[...] = src[...].astype(BF16)

    x = _read_rows(x_refs)
    mod = mod_ref[...]
    if n_o:
        x = x + mod[5:6] * _bdot(_read_rows(o_refs), wo_ref[...])
    shift, scale, gate = (mod[mod_row + k:mod_row + k + 1] for k in range(3))
    h = _modulated(x, g_ref[...], shift, scale).astype(BF16)
    for c in range(FF_CHUNKS):
        lo = c * MXU_N
        gt = _bdot(h, win_ref[:, lo:lo + MXU_N])
        up = _bdot(h, win_ref[:, D_FF + lo:D_FF + lo + MXU_N])
        a_scr[:, lo:lo + MXU_N] = ((gt * jax.nn.sigmoid(gt)) * up).astype(BF16)
    y = x + (0.5 * gate) * _bdot(a_scr[...], wout_ref[...])
    if final_norm:
        y = _rms(y) * fg_ref[...]
    out_ref[...] = y


def _ffn(x, mod, layer, which, g, w_in, w_out, *, n_rows, attn=None, final_g=None, next_w=None):
    const = lambda i: (0, 0)
    x_specs, x_args = _row_specs(x)
    in_specs = x_specs + [pl.BlockSpec((None, None, N_MOD, D), lambda i: (layer, i // LAT_TILES, 0, 0)),
                          _resident((1, D), const),
                          _resident((D, 2 * D_FF), const),
                          _resident((D_FF, D), const)]
    args = x_args + [mod, g.reshape(1, D), w_in, w_out]
    n_o = 0
    if attn is not None:
        o, w_o = attn
        o_specs, o_args = _row_specs(o)
        n_o = len(o_args)
        in_specs += o_specs + [_resident((D, D), const)]
        args += o_args + [w_o]
    if final_g is not None:
        in_specs.append(_resident((1, D), const))
        args.append(final_g.reshape(1, D))
    n_steps = n_rows // TM
    out_specs = [pl.BlockSpec((TM, D), lambda i: (i, 0))]
    out_shape = [jax.ShapeDtypeStruct((n_rows, D), F32)]
    if next_w is not None:
        w_in_all, w_out_all, nl, nw = next_w
        for w_all, slab in ((w_in_all, CAST_IN_ROWS), (w_out_all, CAST_OUT_ROWS)):
            rows, cols = w_all.shape[2:]
            n_slabs = rows // slab
            assert rows % slab == 0 and n_slabs <= n_steps
            in_specs.append(pl.BlockSpec((None, None, slab, cols),
                                         lambda i, n=n_slabs: (nl, nw, jnp.minimum(i, n - 1), 0)))
            args.append(w_all)
            out_specs.append(pl.BlockSpec((slab, cols), lambda i, n=n_slabs: (jnp.minimum(i, n - 1), 0)))
            out_shape.append(jax.ShapeDtypeStruct((rows, cols), BF16))
    outs = pl.pallas_call(
        functools.partial(_ffn_kernel, n_x=len(x_args), n_o=n_o, mod_row=6 * which,
                          final_norm=final_g is not None, cast_next=next_w is not None),
        grid=(n_steps,),
        in_specs=in_specs,
        out_specs=out_specs,
        out_shape=out_shape,
        scratch_shapes=[pltpu.VMEM((TM, D_FF), BF16)],
        compiler_params=_params("arbitrary"),
        name="ffn",
    )(*args)
    return outs[0], tuple(outs[1:])


def _rope_tables(rd, lane0, period):
    t = jnp.arange(SEQ)
    half, nf = rd // 2, rd // 4
    inv_freq = ROPE_THETA ** (-jnp.arange(nf, dtype=F32) / nf)
    ang_r = (t // GRID_W).astype(F32)[:, None] * inv_freq
    ang_c = (t % GRID_W).astype(F32)[:, None] * inv_freq
    zero = jnp.zeros((SEQ, nf), F32)
    cos = jnp.concatenate([jnp.cos(ang_r)] * 2 + [jnp.cos(ang_c)] * 2, axis=1)
    s_up = jnp.concatenate([-jnp.sin(ang_r), zero, -jnp.sin(ang_c), zero], axis=1)
    s_dn = jnp.concatenate([zero, jnp.sin(ang_r), zero, jnp.sin(ang_c)], axis=1)
    assert half == 2 * nf and LANES % period == 0 and lane0 + rd <= period

    def place(blk, fill):
        pat = jnp.full((SEQ, period), fill, F32).at[:, lane0:lane0 + rd].set(blk)
        pat = jnp.tile(pat, (1, LANES // period))
        return jnp.concatenate([pat, jnp.full((TM, LANES), fill, F32)], axis=0)

    return place(cos, 1.0), place(s_up, 0.0), place(s_dn, 0.0)


def _rope(x, cos, s_up, s_dn, q):
    return x * cos + pltpu.roll(x, LANES - q, 1) * s_up + pltpu.roll(x, q, 1) * s_dn


def _rope_specs():
    m = lambda i: (jnp.where(i < BATCH * LAT_TILES, i % LAT_TILES, LAT_TILES), 0)
    return [pl.BlockSpec((TM, LANES), m)] * 3


def _blocks(n_cols):
    return range(0, n_cols, LANES)


def _head_sumsq(x, ones_blk):
    xx = x * x
    hi = xx.astype(BF16)
    lo = (xx - hi.astype(F32)).astype(BF16)
    return _bdot(hi, ones_blk) + _bdot(lo, ones_blk)


def _mixer_input(x_ref, mod_ref, g_ref):
    mod = mod_ref[...]
    return _modulated(x_ref[...], g_ref[...], mod[3:4], mod[4:5]).astype(BF16)


def _qk_tables(c_ref, su_ref, sd_ref):
    k_tabs = (c_ref[...], su_ref[...], sd_ref[...])
    return tuple(t * (HEAD ** -0.5 * LOG2E) for t in k_tabs), k_tabs


def _da_proj_kernel(x_ref, mod_ref, g_ref, w_ref, c_ref, su_ref, sd_ref, out_ref):
    h = _mixer_input(x_ref, mod_ref, g_ref)
    q_tabs, k_tabs = _qk_tables(c_ref, su_ref, sd_ref)
    for c0 in range(0, 2 * D, PROJ_TN):
        y = _bdot(h, w_ref[:, c0:c0 + PROJ_TN])
        for lo in _blocks(PROJ_TN):
            tabs = q_tabs if c0 < D else k_tabs
            out_ref[:, c0 + lo:c0 + lo + LANES] = _rope(y[:, lo:lo + LANES], *tabs, HEAD // 4).astype(BF16)
    out_ref[:, 2 * D:] = _bdot(h, w_ref[:, 2 * D:]).astype(BF16)


def _gqa_proj_kernel(x_ref, mod_ref, g_ref, w_ref, c_ref, su_ref, sd_ref, qg_ref, kg_ref, ones_ref, out_ref):
    h = _mixer_input(x_ref, mod_ref, g_ref)
    q_tabs, k_tabs = _qk_tables(c_ref, su_ref, sd_ref)
    ones_blk = ones_ref[...]
    n_qk = D + GQA_KV * LANES
    for c0 in range(0, n_qk, PROJ_TN):
        y = _bdot(h, w_ref[:, c0:c0 + PROJ_TN])
        tabs, gain = (q_tabs, qg_ref[...]) if c0 < D else (k_tabs, kg_ref[...])
        for m0 in range(0, PROJ_TN, MXU_N):
            ym = y[:, m0:m0 + MXU_N]
            ym = (ym * lax.rsqrt(_head_sumsq(ym, ones_blk) * (1.0 / HEAD) + EPS)) * gain
            for lo in _blocks(MXU_N):
                out_ref[:, c0 + m0 + lo:c0 + m0 + lo + LANES] = _rope(
                    ym[:, lo:lo + LANES], *tabs, HEAD // 4).astype(BF16)
    out_ref[:, n_qk:] = _bdot(h, w_ref[:, n_qk:]).astype(BF16)


def _mla_proj_kernel(x_ref, mod_ref, g_ref, wd_ref, c_ref, su_ref, sd_ref, qg_ref, kvg_ref,
                     wuq_ref, wukv_ref, out_ref):
    h = _mixer_input(x_ref, mod_ref, g_ref)
    c, su, sd = c_ref[...], su_ref[...], sd_ref[...]
    d = _bdot(h, wd_ref[...])
    cq = (_rms(d[:, :MLA_Q_LORA]) * qg_ref[...]).astype(BF16)
    ckv = (_rms(d[:, MLA_Q_LORA:MLA_Q_LORA + MLA_KV_LORA]) * kvg_ref[...]).astype(BF16)
    k_pe = _rope(d[:, MLA_Q_LORA + MLA_KV_LORA:], c, su, sd, MLA_ROPE // 4)
    n_q = MLA_HEADS * LANES
    for c0 in range(0, n_q, PROJ_TN):
        q = _bdot(cq, wuq_ref[:, c0:c0 + PROJ_TN])
        k = _bdot(ckv, wukv_ref[:, c0:c0 + PROJ_TN])
        for lo in _blocks(PROJ_TN):
            qh = _rope(q[:, lo:lo + LANES], c, su, sd, MLA_ROPE // 4) * ((MLA_NOPE + MLA_ROPE) ** -0.5 * LOG2E)
            out_ref[:, c0 + lo:c0 + lo + LANES] = qh.astype(BF16)
            out_ref[:, n_q + c0 + lo:n_q + c0 + lo + LANES] = (k[:, lo:lo + LANES] + k_pe).astype(BF16)
    out_ref[:, 2 * n_q:] = _bdot(ckv, wukv_ref[:, n_q:]).astype(BF16)


def _na_proj_kernel(x_ref, mod_ref, g_ref, w_ref, out_ref):
    h = _mixer_input(x_ref, mod_ref, g_ref)
    out_ref[:, :D] = (_bdot(h, w_ref[:, :D]) * (HEAD ** -0.5 * LOG2E)).astype(BF16)
    out_ref[:, D:] = _bdot(h, w_ref[:, D:]).astype(BF16)


def _project(kernel, x, mod, layer, g, w, extra, extra_specs, out_cols, name):
    const = lambda i: (0, 0)
    in_specs = [pl.BlockSpec((TM, D), lambda i: (i, 0)),
                pl.BlockSpec((None, None, N_MOD, D), lambda i: (layer, i // LAT_TILES, 0, 0)),
                _resident((1, D), const),
                _resident(w.shape, const)] + extra_specs
    return pl.pallas_call(
        kernel,
        grid=(N_TOK // TM,),
        in_specs=in_specs,
        out_specs=pl.BlockSpec((TM, out_cols), lambda i: (i, 0)),
        out_shape=jax.ShapeDtypeStruct((N_TOK, out_cols), BF16),
        compiler_params=_params("arbitrary"),
        name=name,
    )(x, mod, g.reshape(1, D), w, *extra)


def _row(v):
    return v.reshape(1, -1).astype(F32)


def _row_spec(n):
    return _resident((1, n), lambda i: (0, 0))


def _stack_halves(q):
    lane = lax.broadcasted_iota(jnp.int32, (1, LANES), 1)
    m0 = (lane < HEAD).astype(F32)
    qf = q.astype(F32)
    return jnp.concatenate([(qf * m0).astype(BF16), (qf * (1.0 - m0)).astype(BF16)], axis=0)


def _stack_wide(q):
    qa, qb = q[:, :LANES], q[:, LANES:]
    zero = jnp.zeros_like(qa)
    return jnp.concatenate([jnp.concatenate([qa, zero], axis=1), jnp.concatenate([zero, qb], axis=1)], axis=0)


def _pair_out(o_t):
    rows = o_t.shape[1] // 2
    return jnp.concatenate([o_t[:HEAD, :rows], o_t[HEAD:, rows:]], axis=0).T


_Chain = collections.namedtuple("_Chain", "qs segs bias emit")


def _run_chains(chains, s_scr):
    chunks = [[(k, v, lo + c, min(ATT_KC, n - c)) for k, v, lo, n in ch.segs for c in range(0, n, ATT_KC)]
              for ch in chains]
    offs = [[sum(n for _, _, _, n in cl[:j]) for j in range(len(cl))] for cl in chunks]
    stats = [dict(m=None, l=None, acc=None) for _ in chains]

    def score(i, j):
        ch, st = chains[i], stats[i]
        k_ref, _, lo, n = chunks[i][j]
        s = lax.dot_general(k_ref[lo:lo + n, :], ch.qs, NT, preferred_element_type=F32)
        b = ch.bias(offs[i][j], n) if ch.bias is not None else None
        if b is not None:
            s = s + b
        s_scr[i % 2, offs[i][j]:offs[i][j] + n, :] = s
        mj = jnp.max(s, axis=0, keepdims=True)
        st["m"] = mj if st["m"] is None else jnp.maximum(st["m"], mj)

    def value(i, j):
        st = stats[i]
        _, v_ref, lo, n = chunks[i][j]
        e = jnp.exp2(s_scr[i % 2, offs[i][j]:offs[i][j] + n, :] - st["m"])
        lj = jnp.sum(e, axis=0, keepdims=True)
        pv = lax.dot_general(v_ref[lo:lo + n, :], e.astype(BF16), TN, preferred_element_type=F32)
        st["l"] = lj if st["l"] is None else st["l"] + lj
        st["acc"] = pv if st["acc"] is None else st["acc"] + pv

    for i in range(len(chains) + 1):
        n_score = len(chunks[i]) if i < len(chains) else 0
        n_value = len(chunks[i - 1]) if i > 0 else 0
        for j in range(max(n_score, n_value)):
            if j < n_score:
                score(i, j)
            if j < n_value:
                value(i - 1, j)
        if i > 0:
            chains[i - 1].emit(stats[i - 1]["acc"] / stats[i - 1]["l"])


def _attn_refs(refs, n_seg, n_extra):
    kv = refs[1:1 + 2 * n_seg]
    segs = [(kv[2 * s], kv[2 * s + 1], 0, kv[2 * s].shape[0]) for s in range(n_seg)]
    return refs[0], segs, refs[1 + 2 * n_seg:1 + 2 * n_seg + n_extra], refs[-2], refs[-1]


def _attn_kernel(*refs, build, n_seg, n_extra, head_cols):
    q_ref, segs, extras, o_ref, s_scr = _attn_refs(refs, n_seg, n_extra)
    if head_cols is None:
        chains = build(q_ref, segs, extras, o_ref)
    else:
        qc, kc, vc, oc = head_cols
        chains = []
        for hb in range(q_ref.shape[1] // qc):
            view = lambda ref, w: ref.at[:, hb * w:(hb + 1) * w]
            segs_h = [(view(k, kc), view(v, vc), lo, n) for k, v, lo, n in segs]
            chains += build(view(q_ref, qc), segs_h, extras, view(o_ref, oc))
    _run_chains(chains, s_scr)


def _da_chains(q_ref, segs, extras, o_ref, *, lam_init):
    lam_ref, g_ref = extras
    lv = lam_ref[...]
    lam = (jnp.exp(jnp.sum(lv[0:1] * lv[1:2], axis=-1, keepdims=True))
           - jnp.exp(jnp.sum(lv[2:3] * lv[3:4], axis=-1, keepdims=True)) + lam_init)

    def emit(rows):
        def f(o_t):
            o = o_t[:, :SUB_Q] - lam * o_t[:, SUB_Q:]
            o = o * lax.rsqrt(jnp.mean(o * o, axis=0, keepdims=True) + EPS)
            o_ref[rows, :] = ((o.T * g_ref[...]) * (1.0 - lam_init)).astype(BF16)
        return f

    chains = []
    for lo in range(0, q_ref.shape[0], SUB_Q):
        rows = slice(lo, lo + SUB_Q)
        chains.append(_Chain(_stack_halves(q_ref[rows, :]), segs, None, emit(rows)))
    return chains


def _pair_chains(q_ref, segs, extras, o_ref, *, wide_keys):
    def emit(rows, cols):
        def f(o_t):
            o_ref[rows, cols] = _pair_out(o_t).astype(BF16)
        return f

    chains = []
    for lo in range(0, q_ref.shape[0], SUB_Q):
        rows = slice(lo, lo + SUB_Q)
        if wide_keys:
            chains.append(_Chain(_stack_wide(q_ref[rows, :]), segs, None, emit(rows, slice(0, LANES))))
        else:
            for c in _blocks(q_ref.shape[1]):
                cols = slice(c, c + LANES)
                chains.append(_Chain(_stack_halves(q_ref[rows, cols]), segs, None, emit(rows, cols)))
    return chains


def _attention(build, qkv, q_cols, k_cols, v_cols, o_cols, k0, v0, n_hb, extra=(), extra_specs=(), name=""):
    kb, vb = k0 // k_cols, v0 // v_cols
    ctx_blk = N_LAT // CTX
    scratch = lambda n_keys: [pltpu.VMEM((2, n_keys, CHAIN_N), F32)]
    chains_per_head = (SEQ // SUB_Q) * (o_cols // LANES)
    tq = SEQ * min(CHAINS_PER_STEP, chains_per_head) // chains_per_head
    hps = max(1, CHAINS_PER_STEP // chains_per_head)
    n_qt = SEQ // tq
    assert n_hb % hps == 0 and kb % hps == 0 and vb % hps == 0
    kernel = functools.partial(_attn_kernel, build=build, n_extra=len(extra))
    head_cols = (q_cols, k_cols, v_cols, o_cols)
    lat = pl.pallas_call(
        functools.partial(kernel, n_seg=2, head_cols=head_cols if hps > 1 else None),
        grid=(BATCH, n_hb // hps, n_qt),
        in_specs=[pl.BlockSpec((tq, hps * q_cols), lambda b, h, t: (b * n_qt + t, h)),
                  pl.BlockSpec((SEQ, hps * k_cols), lambda b, h, t: (b, kb // hps + h)),
                  pl.BlockSpec((SEQ, hps * v_cols), lambda b, h, t: (b, vb // hps + h)),
                  pl.BlockSpec((CTX, hps * k_cols), lambda b, h, t: (ctx_blk + b, kb // hps + h)),
                  pl.BlockSpec((CTX, hps * v_cols), lambda b, h, t: (ctx_blk + b, vb // hps + h))]
                 + list(extra_specs),
        out_specs=pl.BlockSpec((tq, hps * o_cols), lambda b, h, t: (b * n_qt + t, h)),
        out_shape=jax.ShapeDtypeStruct((N_LAT, D), BF16),
        scratch_shapes=scratch(SEQ + CTX),
        compiler_params=_params("arbitrary", "arbitrary", "arbitrary"),
        name=name,
    )(qkv, qkv, qkv, qkv, qkv, *extra)
    assert n_hb * o_cols == D and k0 % (n_hb * k_cols) == 0 and v0 % (n_hb * v_cols) == 0
    ctx = pl.pallas_call(
        functools.partial(kernel, n_seg=1, head_cols=head_cols),
        grid=(BATCH,),
        in_specs=[pl.BlockSpec((CTX, n_hb * q_cols), lambda b: (ctx_blk + b, 0)),
                  pl.BlockSpec((CTX, n_hb * k_cols), lambda b: (ctx_blk + b, kb // n_hb)),
                  pl.BlockSpec((CTX, n_hb * v_cols), lambda b: (ctx_blk + b, vb // n_hb))] + list(extra_specs),
        out_specs=pl.BlockSpec((CTX, D), lambda b: (b, 0)),
        out_shape=jax.ShapeDtypeStruct((BATCH * CTX, D), BF16),
        scratch_shapes=scratch(CTX),
        compiler_params=_params("arbitrary"),
        name=name + "_ctx",
    )(qkv, qkv, qkv, *extra)
    return lat, ctx


def _na_tile_band(t):
    r0 = t * NA_TILE_ROWS
    return r0, min(max(r0 - NA_WIN_ROWS // 2, 0), GRID_ROWS - NA_TILE_BAND)


def _na_attn_kernel(q_ref, kl_ref, vl_ref, kc_ref, vc_ref, bias_ref, o_ref, s_scr):
    n_band = NA_TILE_BAND * GRID_W
    _, plan = _na_bias_plan()

    def emit(rows, cols):
        def f(o_t):
            o_ref[rows, cols] = _pair_out(o_t).astype(BF16)
        return f

    def band_bias(hp, variant):
        def f(off, n):
            if off + n > n_band:
                return None
            key_rows = [jnp.concatenate([bias_ref[hp, j, plan[variant, kr, p]] for j in range(2)
                                         for p in range(NA_TILE_ROWS // 2)], axis=1)
                        for kr in range(off // GRID_W, (off + n) // GRID_W)]
            return jnp.concatenate(key_rows, axis=0)
        return f

    chains = []
    for hp in range(NA_PAIRS_PER_STEP):
        cols = slice(hp * LANES, (hp + 1) * LANES)
        view = lambda ref: ref.at[:, cols]
        for t in range(NA_TILES):
            r0, b0 = _na_tile_band(t)
            rows = slice(r0 * GRID_W, (r0 + NA_TILE_ROWS) * GRID_W)
            segs = [(view(kl_ref), view(vl_ref), b0 * GRID_W, n_band), (view(kc_ref), view(vc_ref), 0, CTX)]
            chains.append(_Chain(_stack_halves(q_ref[rows, cols]), segs,
                                 band_bias(hp, NA_TILE_VARIANT[t]), emit(rows, cols)))
    _run_chains(chains, s_scr)


def _toeplitz(v):
    lead, w, period = v.shape[:-1], NA_WIN_COLS, 2 * GRID_W - 1
    line = jnp.zeros(lead + (period,), v.dtype)
    line = line.at[..., :w].set(v[..., w - 1:]).at[..., period - (w - 1):].set(v[..., :w - 1])
    x = jnp.broadcast_to(line[..., None, :], lead + (GRID_W, period)).reshape(lead + (GRID_W * period,))
    skew = x[..., :GRID_W * (period - 1)].reshape(lead + (GRID_W, period - 1))
    return jnp.swapaxes(skew[..., :GRID_W], -1, -2)


def _na_bias_plan():
    pairs, plan = [], {}
    for variant, t in enumerate(NA_VARIANT_TILES):
        r0, b0 = _na_tile_band(t)
        for kr in range(NA_TILE_BAND):
            for p in range(NA_TILE_ROWS // 2):
                pair = []
                for r in (r0 + 2 * p, r0 + 2 * p + 1):
                    rs = min(max(r - NA_WIN_ROWS // 2, 0), GRID_ROWS - NA_WIN_ROWS)
                    pair.append(b0 + kr - r + NA_WIN_ROWS - 1 if rs <= b0 + kr < rs + NA_WIN_ROWS else None)
                if tuple(pair) not in pairs:
                    pairs.append(tuple(pair))
                plan[variant, kr, p] = pairs.index(tuple(pair))
    return pairs, plan


def _na_bias(rpb):
    cols = np.arange(GRID_W)
    col_start = np.clip(cols - NA_WIN_COLS // 2, 0, GRID_W - NA_WIN_COLS)
    col_in = (cols[:, None] >= col_start[None, :]) & (cols[:, None] < col_start[None, :] + NA_WIN_COLS)
    blocks = jnp.where(col_in, _toeplitz(rpb) * LOG2E, NEG_INF)
    outside = jnp.full((NA_HEADS, GRID_W, GRID_W), NEG_INF, F32)
    half = lambda x: outside if x is None else blocks[:, x]
    pairs, _ = _na_bias_plan()
    tab = jnp.stack([jnp.concatenate([half(x0), half(x1)], axis=-1) for x0, x1 in pairs], axis=1)
    return tab.reshape(NA_HEADS // 2, 2, len(pairs), GRID_W, LANES).astype(F32)


def _na_attention(qkv, bias):
    w = NA_PAIRS_PER_STEP * LANES
    n_hb = D // w
    kb, vb = D // w, 2 * D // w
    in_specs = [pl.BlockSpec((SEQ, w), lambda h, b: (b, h)),
                pl.BlockSpec((SEQ, w), lambda h, b: (b, kb + h)),
                pl.BlockSpec((SEQ, w), lambda h, b: (b, vb + h)),
                pl.BlockSpec((CTX, w), lambda h, b: (N_LAT // CTX + b, kb + h)),
                pl.BlockSpec((CTX, w), lambda h, b: (N_LAT // CTX + b, vb + h)),
                pl.BlockSpec((NA_PAIRS_PER_STEP,) + bias.shape[1:], lambda h, b: (h, 0, 0, 0, 0))]
    return pl.pallas_call(
        _na_attn_kernel,
        grid=(n_hb, BATCH),
        in_specs=in_specs,
        out_specs=pl.BlockSpec((SEQ, w), lambda h, b: (b, h)),
        out_shape=jax.ShapeDtypeStruct((N_LAT, D), BF16),
        scratch_shapes=[pltpu.VMEM((2, NA_TILE_BAND * GRID_W + CTX, CHAIN_N), F32)],
        compiler_params=_params("arbitrary", "arbitrary"),
        name="na_attention",
    )(qkv, qkv, qkv, qkv, qkv, bias)


def _dup_heads(w, n_heads):
    w = w.reshape(D, n_heads, 1, HEAD)
    return jnp.broadcast_to(w, (D, n_heads, LANES // HEAD, HEAD)).reshape(D, n_heads * LANES)


def _pad_heads(w, width):
    k = w.shape[0]
    w = w.reshape(k, MLA_HEADS, width)
    return jnp.pad(w, ((0, 0), (0, 0), (0, LANES - width))).reshape(k, MLA_HEADS * LANES)


def kernel(x, c, ctx, c_ctx, w_mod, b_mod, norm_g, w_ffn_in, w_ffn_out, da_w_qkv, da_lam_q1, da_lam_k1, da_lam_q2, da_lam_k2, da_subln_g, da_w_o, gqa_w_qkv, gqa_q_norm_g, gqa_k_norm_g, gqa_w_o, mla_w_down, mla_q_norm_g, mla_kv_norm_g, mla_w_uq, mla_w_ukv, mla_w_o, na_w_qkv, na_rpb, na_w_o, final_g):
    assert x.shape == (BATCH, SEQ, D) and ctx.shape == (BATCH, CTX, D) and w_mod.shape[0] == DEPTH == 4

    cvec = jnp.zeros((MOD_ROWS, D), F32).at[:BATCH].set(c).at[BATCH].set(c_ctx)
    mod = _modulation(cvec, w_mod, b_mod).reshape(DEPTH, MOD_ROWS, N_MOD, D)
    xs = (x.reshape(N_LAT, D), ctx.reshape(BATCH * CTX, D))
    w_ffn = (w_ffn_in[0, 0].astype(BF16), w_ffn_out[0, 0].astype(BF16))

    rope64 = _rope_tables(HEAD, 0, HEAD)
    rope_mla = _rope_tables(MLA_ROPE, MLA_NOPE, LANES)
    ones_blk = jnp.kron(jnp.eye(MXU_N // HEAD, dtype=F32), jnp.ones((HEAD, HEAD), F32)).astype(BF16)
    blk_spec = _resident((MXU_N, MXU_N), lambda i: (0, 0))

    for layer in range(DEPTH):
        last = layer == DEPTH - 1
        xs, w_ffn = _ffn(xs, mod, layer, 0, norm_g[layer, 0], *w_ffn, n_rows=N_TOK,
                         next_w=(w_ffn_in, w_ffn_out, layer, 1))
        g_mix = norm_g[layer, 1]
        if layer == 0:
            qkv = _project(_da_proj_kernel, xs, mod, layer, g_mix, da_w_qkv[0].astype(BF16),
                           list(rope64), _rope_specs(), 3 * D, "da_proj")
            lam_init = 0.8 - 0.6 * math.exp(-0.3 * layer)
            lam = jnp.zeros((8, LANES), F32).at[:4, :HEAD].set(
                jnp.stack([da_lam_q1[0], da_lam_k1[0], da_lam_q2[0], da_lam_k2[0]]))
            const3 = lambda *_: (0, 0)
            o = _attention(functools.partial(_da_chains, lam_init=lam_init), qkv,
                           LANES, LANES, LANES, LANES, D, 2 * D, DA_HEADS,
                           extra=(lam, _row(da_subln_g[0])),
                           extra_specs=(pl.BlockSpec((8, LANES), const3), pl.BlockSpec((1, LANES), const3)),
                           name="da_attention")
            w_o = da_w_o[0]
        elif layer == 1:
            wq, wk, wv = jnp.split(gqa_w_qkv[0], [GQA_HEADS * HEAD, (GQA_HEADS + GQA_KV) * HEAD], axis=1)
            w = jnp.concatenate([wq, _dup_heads(wk, GQA_KV), _dup_heads(wv, GQA_KV)], axis=1).astype(BF16)
            gains = (_row(jnp.tile(gqa_q_norm_g[0], MXU_N // HEAD)), _row(jnp.tile(gqa_k_norm_g[0], MXU_N // HEAD)))
            qkv = _project(_gqa_proj_kernel, xs, mod, layer, g_mix, w,
                           list(rope64) + list(gains) + [ones_blk],
                           _rope_specs() + [_row_spec(MXU_N)] * 2 + [blk_spec],
                           D + 2 * GQA_KV * LANES, "gqa_proj")
            o = _attention(functools.partial(_pair_chains, wide_keys=False), qkv,
                           2 * LANES, LANES, LANES, 2 * LANES, D, D + GQA_KV * LANES, GQA_KV,
                           name="gqa_attention")
            w_o = gqa_w_o[0]
        elif layer == 2:
            wd = mla_w_down[0]
            n_lora = MLA_Q_LORA + MLA_KV_LORA
            wd = jnp.concatenate([wd[:, :n_lora], jnp.zeros((D, MLA_NOPE), F32), wd[:, n_lora:],
                                  jnp.zeros((D, LANES - MLA_NOPE - MLA_ROPE), F32)], axis=1).astype(BF16)
            wuq = _pad_heads(mla_w_uq[0], MLA_NOPE + MLA_ROPE).astype(BF16)
            wukv = mla_w_ukv[0].reshape(MLA_KV_LORA, MLA_HEADS, 2 * HEAD)
            wuk = _pad_heads(wukv[:, :, :MLA_NOPE].reshape(MLA_KV_LORA, -1), MLA_NOPE)
            wuv = wukv[:, :, MLA_NOPE:].reshape(MLA_KV_LORA, MLA_HEADS * HEAD)
            wukv = jnp.concatenate([wuk, wuv], axis=1).astype(BF16)
            n_q = MLA_HEADS * LANES
            const = lambda i: (0, 0)
            qkv = _project(_mla_proj_kernel, xs, mod, layer, g_mix, wd,
                           list(rope_mla) + [_row(mla_q_norm_g[0]), _row(mla_kv_norm_g[0]), wuq, wukv],
                           _rope_specs() + [_row_spec(MLA_Q_LORA), _row_spec(MLA_KV_LORA),
                                            _resident(wuq.shape, const), _resident(wukv.shape, const)],
                           2 * n_q + D, "mla_proj")
            o = _attention(functools.partial(_pair_chains, wide_keys=True), qkv,
                           2 * LANES, 2 * LANES, LANES, LANES, n_q, 2 * n_q, MLA_HEADS // 2,
                           name="mla_attention")
            w_o = mla_w_o[0]
        else:
            qkv = _project(_na_proj_kernel, xs, mod, layer, g_mix, na_w_qkv[0].astype(BF16), [], [], 3 * D, "na_proj")
            o = _na_attention(qkv, _na_bias(na_rpb[0]))
            w_o = na_w_o[0]
        xs, w_ffn = _ffn(xs, mod, layer, 1, norm_g[layer, 2], *w_ffn, n_rows=N_LAT if last else N_TOK,
                         attn=(o, w_o.astype(BF16)), final_g=final_g if last else None,
                         next_w=None if last else (w_ffn_in, w_ffn_out, layer + 1, 0))
    return xs.reshape(BATCH, SEQ, D)
```

```python
import collections
import functools
import math

import jax
import jax.numpy as jnp
import numpy as np
from jax import lax
from jax.experimental import pallas as pl
from jax.experimental.pallas import tpu as pltpu

F32 = jnp.float32
BF16 = jnp.bfloat16

D = 1024
BATCH = 8
SEQ = 2048
DEPTH = 4
GRID_W = 64
GRID_ROWS = SEQ // GRID_W
CTX = 256
N_MOD = 9
D_FF = 2816
ROPE_THETA = 10000.0
NEG_INF = -1e30
EPS = 1e-6
LOG2E = math.log2(math.e)

N_LAT = BATCH * SEQ
N_TOK = N_LAT + BATCH * CTX
N_GROUPS = N_TOK // SEQ
MOD_ROWS = 16
assert BATCH * CTX == SEQ and N_GROUPS == BATCH + 1 and N_GROUPS <= MOD_ROWS

LANES = 128
MXU_N = 256
HEAD = 64
FF_CHUNKS = D_FF // MXU_N
assert FF_CHUNKS * MXU_N == D_FF

DA_HEADS = 8
GQA_HEADS, GQA_KV = 16, 4
MLA_HEADS, MLA_Q_LORA, MLA_KV_LORA, MLA_NOPE, MLA_ROPE = 16, 256, 128, 64, 32
NA_HEADS, NA_WIN_ROWS, NA_WIN_COLS = 16, 8, 16
NA_TILE_ROWS = 4
NA_TILE_BAND = 12
NA_TILES = GRID_ROWS // NA_TILE_ROWS
NA_PAIRS_PER_STEP = 2
NA_VARIANT_TILES = (0, 1, NA_TILES - 1)
NA_TILE_VARIANT = tuple(0 if t == 0 else 2 if t == NA_TILES - 1 else 1 for t in range(NA_TILES))
assert NA_TILE_ROWS + NA_WIN_ROWS <= NA_TILE_BAND and (NA_TILE_BAND * GRID_W) % MXU_N == 0

TM = 512
CHAINS_PER_STEP = 16
SUB_Q = 256
CHAIN_N = 2 * SUB_Q
ATT_KC = 1024
MOD_TN = 2304
PROJ_TN = 512
CAST_IN_ROWS, CAST_OUT_ROWS = 32, 176
LAT_TILES = SEQ // TM
assert SUB_Q == CTX == NA_TILE_ROWS * GRID_W
VMEM_LIMIT = 56 * 1024 * 1024

TN = (((0,), (0,)), ((), ()))


def _params(*sem):
    return pltpu.CompilerParams(dimension_semantics=sem, vmem_limit_bytes=VMEM_LIMIT)


def _resident(shape, index_map):
    return pl.BlockSpec(shape, index_map, pipeline_mode=pl.Buffered(1))


def _bdot(a, b):
    return jnp.dot(a, b, preferred_element_type=F32)


def _rms(x):
    return x * lax.rsqrt(jnp.mean(x * x, axis=-1, keepdims=True) + EPS)


def _modulated(x, g, shift, scale):
    return (_rms(x) * g) * (1.0 + scale) + shift


def _mod_kernel(c_ref, w_ref, b_ref, o_ref):
    c = c_ref[...]
    s = (c * jax.nn.sigmoid(c)).astype(BF16)
    o_ref[...] = _bdot(s, w_ref[...].astype(BF16)) + b_ref[...]


def _modulation(cvec, w_mod, b_mod):
    n = N_MOD * D
    return pl.pallas_call(
        _mod_kernel,
        grid=(DEPTH, n // MOD_TN),
        in_specs=[pl.BlockSpec((MOD_ROWS, D), lambda l, j: (0, 0)),
                  pl.BlockSpec((None, D, MOD_TN), lambda l, j: (l, 0, j)),
                  pl.BlockSpec((None, 1, MOD_TN), lambda l, j: (l, 0, j))],
        out_specs=pl.BlockSpec((None, MOD_ROWS, MOD_TN), lambda l, j: (l, 0, j)),
        out_shape=jax.ShapeDtypeStruct((DEPTH, MOD_ROWS, n), F32),
        compiler_params=_params("arbitrary", "arbitrary"),
        name="modulation",
    )(cvec, w_mod, b_mod.reshape(DEPTH, 1, n))


def _row_specs(src):
    if not isinstance(src, tuple):
        return [pl.BlockSpec((TM, src.shape[1]), lambda i: (i, 0))], [src]
    lat, ctx = src
    n_lat = lat.shape[0] // TM
    return ([pl.BlockSpec((TM, lat.shape[1]), lambda i: (jnp.minimum(i, n_lat - 1), 0)),
             pl.BlockSpec((TM, ctx.shape[1]), lambda i: (jnp.maximum(i - n_lat, 0), 0))], [lat, ctx])


def _read_rows(refs):
    if len(refs) == 1:
        return refs[0][...]
    return jnp.where(pl.program_id(0) < N_LAT // TM, refs[0][...], refs[1][...])


def _ffn_kernel(*refs, n_x, n_o, mod_row, final_norm, cast_next):
    it = iter(refs)
    x_refs = [next(it) for _ in range(n_x)]
    mod_ref, g_ref, win_ref, wout_ref = (next(it) for _ in range(4))
    o_refs = [next(it) for _ in range(n_o)]
    wo_ref = next(it) if n_o else None
    fg_ref = next(it) if final_norm else None
    cast_in = [next(it) for _ in range(2 * cast_next)]
    out_ref = next(it)
    cast_out = [next(it) for _ in range(2 * cast_next)]
    a_scr = next(it)
    for src, dst in zip(cast_in, cast_out):
        dst[...] = src[...].astype(BF16)

    x = _read_rows(x_refs)
    mod = mod_ref[...]
    if n_o:
        x = x + mod[5:6] * _bdot(_read_rows(o_refs), wo_ref[...])
    shift, scale, gate = (mod[mod_row + k:mod_row + k + 1] for k in range(3))
    h = _modulated(x, g_ref[...], shift, scale).astype(BF16)
    for c in range(FF_CHUNKS):
        lo = c * MXU_N
        gt = _bdot(h, win_ref[:, lo:lo + MXU_N])
        up = _bdot(h, win_ref[:, D_FF + lo:D_FF + lo + MXU_N])
        a_scr[:, lo:lo + MXU_N] = ((gt * jax.nn.sigmoid(gt)) * up).astype(BF16)
    y = x + (0.5 * gate) * _bdot(a_scr[...], wout_ref[...])
    if final_norm:
        y = _rms(y) * fg_ref[...]
    out_ref[...] = y


def _ffn(x, mod, layer, which, g, w_in, w_out, *, n_rows, attn=None, final_g=None, next_w=None):
    const = lambda i: (0, 0)
    x_specs, x_args = _row_specs(x)
    in_specs = x_specs + [pl.BlockSpec((None, None, N_MOD, D), lambda i: (layer, i // LAT_TILES, 0, 0)),
                          _resident((1, D), const),
                          _resident((D, 2 * D_FF), const),
                          _resident((D_FF, D), const)]
    args = x_args + [mod, g.reshape(1, D), w_in, w_out]
    n_o = 0
    if attn is not None:
        o, w_o = attn
        o_specs, o_args = _row_specs(o)
        n_o = len(o_args)
        in_specs += o_specs + [_resident((D, D), const)]
        args += o_args + [w_o]
    if final_g is not None:
        in_specs.append(_resident((1, D), const))
        args.append(final_g.reshape(1, D))
    n_steps = n_rows // TM
    out_specs = [pl.BlockSpec((TM, D), lambda i: (i, 0))]
    out_shape = [jax.ShapeDtypeStruct((n_rows, D), F32)]
    if next_w is not None:
        w_in_all, w_out_all, nl, nw = next_w
        for w_all, slab in ((w_in_all, CAST_IN_ROWS), (w_out_all, CAST_OUT_ROWS)):
            rows, cols = w_all.shape[2:]
            n_slabs = rows // slab
            assert rows % slab == 0 and n_slabs <= n_steps
            in_specs.append(pl.BlockSpec((None, None, slab, cols),
                                         lambda i, n=n_slabs: (nl, nw, jnp.minimum(i, n - 1), 0)))
            args.append(w_all)
            out_specs.append(pl.BlockSpec((slab, cols), lambda i, n=n_slabs: (jnp.minimum(i, n - 1), 0)))
            out_shape.append(jax.ShapeDtypeStruct((rows, cols), BF16))
    outs = pl.pallas_call(
        functools.partial(_ffn_kernel, n_x=len(x_args), n_o=n_o, mod_row=6 * which,
                          final_norm=final_g is not None, cast_next=next_w is not None),
        grid=(n_steps,),
        in_specs=in_specs,
        out_specs=out_specs,
        out_shape=out_shape,
        scratch_shapes=[pltpu.VMEM((TM, D_FF), BF16)],
        compiler_params=_params("arbitrary"),
        name="ffn",
    )(*args)
    return outs[0], tuple(outs[1:])


def _rope_tables(rd, lane0, period):
    t = jnp.arange(SEQ)
    half, nf = rd // 2, rd // 4
    inv_freq = ROPE_THETA ** (-jnp.arange(nf, dtype=F32) / nf)
    ang_r = (t // GRID_W).astype(F32)[:, None] * inv_freq
    ang_c = (t % GRID_W).astype(F32)[:, None] * inv_freq
    zero = jnp.zeros((SEQ, nf), F32)
    cos = jnp.concatenate([jnp.cos(ang_r)] * 2 + [jnp.cos(ang_c)] * 2, axis=1)
    s_up = jnp.concatenate([-jnp.sin(ang_r), zero, -jnp.sin(ang_c), zero], axis=1)
    s_dn = jnp.concatenate([zero, jnp.sin(ang_r), zero, jnp.sin(ang_c)], axis=1)
    assert half == 2 * nf and LANES % period == 0 and lane0 + rd <= period

    def place(blk, fill):
        pat = jnp.full((SEQ, period), fill, F32).at[:, lane0:lane0 + rd].set(blk)
        pat = jnp.tile(pat, (1, LANES // period))
        return jnp.concatenate([pat, jnp.full((TM, LANES), fill, F32)], axis=0)

    return place(cos, 1.0), place(s_up, 0.0), place(s_dn, 0.0)


def _rope(x, cos, s_up, s_dn, q):
    return x * cos + pltpu.roll(x, LANES - q, 1) * s_up + pltpu.roll(x, q, 1) * s_dn


def _rope_specs():
    m = lambda i: (jnp.where(i < BATCH * LAT_TILES, i % LAT_TILES, LAT_TILES), 0)
    return [pl.BlockSpec((TM, LANES), m)] * 3


def _blocks(n_cols):
    return range(0, n_cols, LANES)


def _head_sumsq(x, ones_blk):
    xx = x * x
    hi = xx.astype(BF16)
    lo = (xx - hi.astype(F32)).astype(BF16)
    return _bdot(hi, ones_blk) + _bdot(lo, ones_blk)


def _mixer_input(x_ref, mod_ref, g_ref):
    mod = mod_ref[...]
    return _modulated(x_ref[...], g_ref[...], mod[3:4], mod[4:5]).astype(BF16)


def _qk_tables(c_ref, su_ref, sd_ref):
    k_tabs = (c_ref[...], su_ref[...], sd_ref[...])
    return tuple(t * (HEAD ** -0.5 * LOG2E) for t in k_tabs), k_tabs


def _da_proj_kernel(x_ref, mod_ref, g_ref, w_ref, c_ref, su_ref, sd_ref, out_ref):
    h = _mixer_input(x_ref, mod_ref, g_ref)
    q_tabs, k_tabs = _qk_tables(c_ref, su_ref, sd_ref)
    for c0 in range(0, 2 * D, PROJ_TN):
        y = _bdot(h, w_ref[:, c0:c0 + PROJ_TN])
        for lo in _blocks(PROJ_TN):
            tabs = q_tabs if c0 < D else k_tabs
            out_ref[:, c0 + lo:c0 + lo + LANES] = _rope(y[:, lo:lo + LANES], *tabs, HEAD // 4).astype(BF16)
    out_ref[:, 2 * D:] = _bdot(h, w_ref[:, 2 * D:]).astype(BF16)


def _gqa_proj_kernel(x_ref, mod_ref, g_ref, w_ref, c_ref, su_ref, sd_ref, qg_ref, kg_ref, ones_ref, out_ref):
    h = _mixer_input(x_ref, mod_ref, g_ref)
    q_tabs, k_tabs = _qk_tables(c_ref, su_ref, sd_ref)
    ones_blk = ones_ref[...]
    n_qk = D + GQA_KV * LANES
    for c0 in range(0, n_qk, PROJ_TN):
        y = _bdot(h, w_ref[:, c0:c0 + PROJ_TN])
        tabs, gain = (q_tabs, qg_ref[...]) if c0 < D else (k_tabs, kg_ref[...])
        for m0 in range(0, PROJ_TN, MXU_N):
            ym = y[:, m0:m0 + MXU_N]
            ym = (ym * lax.rsqrt(_head_sumsq(ym, ones_blk) * (1.0 / HEAD) + EPS)) * gain
            for lo in _blocks(MXU_N):
                out_ref[:, c0 + m0 + lo:c0 + m0 + lo + LANES] = _rope(
                    ym[:, lo:lo + LANES], *tabs, HEAD // 4).astype(BF16)
    out_ref[:, n_qk:] = _bdot(h, w_ref[:, n_qk:]).astype(BF16)


def _mla_proj_kernel(x_ref, mod_ref, g_ref, wd_ref, c_ref, su_ref, sd_ref, qg_ref, kvg_ref,
                     wuq_ref, wukv_ref, out_ref):
    h = _mixer_input(x_ref, mod_ref, g_ref)
    c, su, sd = c_ref[...], su_ref[...], sd_ref[...]
    d = _bdot(h, wd_ref[...])
    cq = (_rms(d[:, :MLA_Q_LORA]) * qg_ref[...]).astype(BF16)
    ckv = (_rms(d[:, MLA_Q_LORA:MLA_Q_LORA + MLA_KV_LORA]) * kvg_ref[...]).astype(BF16)
    k_pe = _rope(d[:, MLA_Q_LORA + MLA_KV_LORA:], c, su, sd, MLA_ROPE // 4)
    n_q = MLA_HEADS * LANES
    for c0 in range(0, n_q, PROJ_TN):
        q = _bdot(cq, wuq_ref[:, c0:c0 + PROJ_TN])
        k = _bdot(ckv, wukv_ref[:, c0:c0 + PROJ_TN])
        for lo in _blocks(PROJ_TN):
            qh = _rope(q[:, lo:lo + LANES], c, su, sd, MLA_ROPE // 4) * ((MLA_NOPE + MLA_ROPE) ** -0.5 * LOG2E)
            out_ref[:, c0 + lo:c0 + lo + LANES] = qh.astype(BF16)
            out_ref[:, n_q + c0 + lo:n_q + c0 + lo + LANES] = (k[:, lo:lo + LANES] + k_pe).astype(BF16)
    out_ref[:, 2 * n_q:] = _bdot(ckv, wukv_ref[:, n_q:]).astype(BF16)


def _na_proj_kernel(x_ref, mod_ref, g_ref, w_ref, out_ref):
    h = _mixer_input(x_ref, mod_ref, g_ref)
    out_ref[:, :D] = (_bdot(h, w_ref[:, :D]) * (HEAD ** -0.5 * LOG2E)).astype(BF16)
    out_ref[:, D:] = _bdot(h, w_ref[:, D:]).astype(BF16)


def _project(kernel, x, mod, layer, g, w, extra, extra_specs, out_cols, name):
    const = lambda i: (0, 0)
    in_specs = [pl.BlockSpec((TM, D), lambda i: (i, 0)),
                pl.BlockSpec((None, None, N_MOD, D), lambda i: (layer, i // LAT_TILES, 0, 0)),
                _resident((1, D), const),
                _resident(w.shape, const)] + extra_specs
    return pl.pallas_call(
        kernel,
        grid=(N_TOK // TM,),
        in_specs=in_specs,
        out_specs=pl.BlockSpec((TM, out_cols), lambda i: (i, 0)),
        out_shape=jax.ShapeDtypeStruct((N_TOK, out_cols), BF16),
        compiler_params=_params("arbitrary"),
        name=name,
    )(x, mod, g.reshape(1, D), w, *extra)


def _row(v):
    return v.reshape(1, -1).astype(F32)


def _row_spec(n):
    return _resident((1, n), lambda i: (0, 0))


def _stack_halves(q):
    dim = lax.broadcasted_iota(jnp.int32, (LANES, 1), 0)
    m0 = (dim < HEAD).astype(F32)
    qt = q.astype(F32).T
    return jnp.concatenate([(qt * m0).astype(BF16), (qt * (1.0 - m0)).astype(BF16)], axis=1)


def _stack_wide(q):
    qt = q.astype(F32).T.astype(BF16)
    qa, qb = qt[:LANES], qt[LANES:]
    zero = jnp.zeros_like(qa)
    return jnp.concatenate([jnp.concatenate([qa, zero], axis=0), jnp.concatenate([zero, qb], axis=0)], axis=1)


def _pair_out(o_t):
    rows = o_t.shape[1] // 2
    return jnp.concatenate([o_t[:HEAD, :rows], o_t[HEAD:, rows:]], axis=0).T


_Chain = collections.namedtuple("_Chain", "qs segs bias emit")


def _run_chains(chains, s_scr):
    chunks = [[(k, v, lo + c, min(ATT_KC, n - c)) for k, v, lo, n in ch.segs for c in range(0, n, ATT_KC)]
              for ch in chains]
    offs = [[sum(n for _, _, _, n in cl[:j]) for j in range(len(cl))] for cl in chunks]
    stats = [dict(m=None, l=None, acc=None) for _ in chains]

    def score(i, j):
        ch, st = chains[i], stats[i]
        k_ref, _, lo, n = chunks[i][j]
        s = _bdot(k_ref[lo:lo + n, :], ch.qs)
        b = ch.bias(offs[i][j], n) if ch.bias is not None else None
        if b is not None:
            s = s + b
        s_scr[i % 2, offs[i][j]:offs[i][j] + n, :] = s
        mj = jnp.max(s, axis=0, keepdims=True)
        st["m"] = mj if st["m"] is None else jnp.maximum(st["m"], mj)

    def value(i, j):
        st = stats[i]
        _, v_ref, lo, n = chunks[i][j]
        e = jnp.exp2(s_scr[i % 2, offs[i][j]:offs[i][j] + n, :] - st["m"])
        lj = jnp.sum(e, axis=0, keepdims=True)
        pv = lax.dot_general(v_ref[lo:lo + n, :], e.astype(BF16), TN, preferred_element_type=F32)
        st["l"] = lj if st["l"] is None else st["l"] + lj
        st["acc"] = pv if st["acc"] is None else st["acc"] + pv

    for i in range(len(chains) + 1):
        n_score = len(chunks[i]) if i < len(chains) else 0
        n_value = len(chunks[i - 1]) if i > 0 else 0
        for j in range(max(n_score, n_value)):
            if j < n_score:
                score(i, j)
            if j < n_value:
                value(i - 1, j)
        if i > 0:
            chains[i - 1].emit(stats[i - 1]["acc"] / stats[i - 1]["l"])


def _attn_refs(refs, n_seg, n_extra):
    kv = refs[1:1 + 2 * n_seg]
    segs = [(kv[2 * s], kv[2 * s + 1], 0, kv[2 * s].shape[0]) for s in range(n_seg)]
    return refs[0], segs, refs[1 + 2 * n_seg:1 + 2 * n_seg + n_extra], refs[-2], refs[-1]


def _attn_kernel(*refs, build, n_seg, n_extra, head_cols):
    q_ref, segs, extras, o_ref, s_scr = _attn_refs(refs, n_seg, n_extra)
    if head_cols is None:
        chains = build(q_ref, segs, extras, o_ref)
    else:
        qc, kc, vc, oc = head_cols
        chains = []
        for hb in range(q_ref.shape[1] // qc):
            view = lambda ref, w: ref.at[:, hb * w:(hb + 1) * w]
            segs_h = [(view(k, kc), view(v, vc), lo, n) for k, v, lo, n in segs]
            chains += build(view(q_ref, qc), segs_h, extras, view(o_ref, oc))
    _run_chains(chains, s_scr)


def _da_chains(q_ref, segs, extras, o_ref, *, lam_init):
    lam_ref, g_ref = extras
    lv = lam_ref[...]
    lam = (jnp.exp(jnp.sum(lv[0:1] * lv[1:2], axis=-1, keepdims=True))
           - jnp.exp(jnp.sum(lv[2:3] * lv[3:4], axis=-1, keepdims=True)) + lam_init)

    def emit(rows):
        def f(o_t):
            o = o_t[:, :SUB_Q] - lam * o_t[:, SUB_Q:]
            o = o * lax.rsqrt(jnp.mean(o * o, axis=0, keepdims=True) + EPS)
            o_ref[rows, :] = ((o.T * g_ref[...]) * (1.0 - lam_init)).astype(BF16)
        return f

    chains = []
    for lo in range(0, q_ref.shape[0], SUB_Q):
        rows = slice(lo, lo + SUB_Q)
        chains.append(_Chain(_stack_halves(q_ref[rows, :]), segs, None, emit(rows)))
    return chains


def _pair_chains(q_ref, segs, extras, o_ref, *, wide_keys):
    def emit(rows, cols):
        def f(o_t):
            o_ref[rows, cols] = _pair_out(o_t).astype(BF16)
        return f

    chains = []
    for lo in range(0, q_ref.shape[0], SUB_Q):
        rows = slice(lo, lo + SUB_Q)
        if wide_keys:
            chains.append(_Chain(_stack_wide(q_ref[rows, :]), segs, None, emit(rows, slice(0, LANES))))
        else:
            for c in _blocks(q_ref.shape[1]):
                cols = slice(c, c + LANES)
                chains.append(_Chain(_stack_halves(q_ref[rows, cols]), segs, None, emit(rows, cols)))
    return chains


def _attention(build, qkv, q_cols, k_cols, v_cols, o_cols, k0, v0, n_hb, extra=(), extra_specs=(), name=""):
    kb, vb = k0 // k_cols, v0 // v_cols
    ctx_blk = N_LAT // CTX
    scratch = lambda n_keys: [pltpu.VMEM((2, n_keys, CHAIN_N), F32)]
    chains_per_head = (SEQ // SUB_Q) * (o_cols // LANES)
    tq = SEQ * min(CHAINS_PER_STEP, chains_per_head) // chains_per_head
    hps = max(1, CHAINS_PER_STEP // chains_per_head)
    n_qt = SEQ // tq
    assert n_hb % hps == 0 and kb % hps == 0 and vb % hps == 0
    kernel = functools.partial(_attn_kernel, build=build, n_extra=len(extra))
    head_cols = (q_cols, k_cols, v_cols, o_cols)
    lat = pl.pallas_call(
        functools.partial(kernel, n_seg=2, head_cols=head_cols if hps > 1 else None),
        grid=(BATCH, n_hb // hps, n_qt),
        in_specs=[pl.BlockSpec((tq, hps * q_cols), lambda b, h, t: (b * n_qt + t, h)),
                  pl.BlockSpec((SEQ, hps * k_cols), lambda b, h, t: (b, kb // hps + h)),
                  pl.BlockSpec((SEQ, hps * v_cols), lambda b, h, t: (b, vb // hps + h)),
                  pl.BlockSpec((CTX, hps * k_cols), lambda b, h, t: (ctx_blk + b, kb // hps + h)),
                  pl.BlockSpec((CTX, hps * v_cols), lambda b, h, t: (ctx_blk + b, vb // hps + h))]
                 + list(extra_specs),
        out_specs=pl.BlockSpec((tq, hps * o_cols), lambda b, h, t: (b * n_qt + t, h)),
        out_shape=jax.ShapeDtypeStruct((N_LAT, D), BF16),
        scratch_shapes=scratch(SEQ + CTX),
        compiler_params=_params("arbitrary", "arbitrary", "arbitrary"),
        name=name,
    )(qkv, qkv, qkv, qkv, qkv, *extra)
    assert n_hb * o_cols == D and k0 % (n_hb * k_cols) == 0 and v0 % (n_hb * v_cols) == 0
    ctx = pl.pallas_call(
        functools.partial(kernel, n_seg=1, head_cols=head_cols),
        grid=(BATCH,),
        in_specs=[pl.BlockSpec((CTX, n_hb * q_cols), lambda b: (ctx_blk + b, 0)),
                  pl.BlockSpec((CTX, n_hb * k_cols), lambda b: (ctx_blk + b, kb // n_hb)),
                  pl.BlockSpec((CTX, n_hb * v_cols), lambda b: (ctx_blk + b, vb // n_hb))] + list(extra_specs),
        out_specs=pl.BlockSpec((CTX, D), lambda b: (b, 0)),
        out_shape=jax.ShapeDtypeStruct((BATCH * CTX, D), BF16),
        scratch_shapes=scratch(CTX),
        compiler_params=_params("arbitrary"),
        name=name + "_ctx",
    )(qkv, qkv, qkv, *extra)
    return lat, ctx


def _na_tile_band(t):
    r0 = t * NA_TILE_ROWS
    return r0, min(max(r0 - NA_WIN_ROWS // 2, 0), GRID_ROWS - NA_TILE_BAND)


def _na_attn_kernel(q_ref, kl_ref, vl_ref, kc_ref, vc_ref, bias_ref, o_ref, s_scr):
    n_band = NA_TILE_BAND * GRID_W
    _, plan = _na_bias_plan()

    def emit(rows, cols):
        def f(o_t):
            o_ref[rows, cols] = _pair_out(o_t).astype(BF16)
        return f

    def band_bias(hp, variant):
        def f(off, n):
            if off + n > n_band:
                return None
            key_rows = [jnp.concatenate([bias_ref[hp, j, plan[variant, kr, p]] for j in range(2)
                                         for p in range(NA_TILE_ROWS // 2)], axis=1)
                        for kr in range(off // GRID_W, (off + n) // GRID_W)]
            return jnp.concatenate(key_rows, axis=0)
        return f

    chains = []
    for hp in range(NA_PAIRS_PER_STEP):
        cols = slice(hp * LANES, (hp + 1) * LANES)
        view = lambda ref: ref.at[:, cols]
        for t in range(NA_TILES):
            r0, b0 = _na_tile_band(t)
            rows = slice(r0 * GRID_W, (r0 + NA_TILE_ROWS) * GRID_W)
            segs = [(view(kl_ref), view(vl_ref), b0 * GRID_W, n_band), (view(kc_ref), view(vc_ref), 0, CTX)]
            chains.append(_Chain(_stack_halves(q_ref[rows, cols]), segs,
                                 band_bias(hp, NA_TILE_VARIANT[t]), emit(rows, cols)))
    _run_chains(chains, s_scr)


def _toeplitz(v):
    lead, w, period = v.shape[:-1], NA_WIN_COLS, 2 * GRID_W - 1
    line = jnp.zeros(lead + (period,), v.dtype)
    line = line.at[..., :w].set(v[..., w - 1:]).at[..., period - (w - 1):].set(v[..., :w - 1])
    x = jnp.broadcast_to(line[..., None, :], lead + (GRID_W, period)).reshape(lead + (GRID_W * period,))
    skew = x[..., :GRID_W * (period - 1)].reshape(lead + (GRID_W, period - 1))
    return jnp.swapaxes(skew[..., :GRID_W], -1, -2)


def _na_bias_plan():
    pairs, plan = [], {}
    for variant, t in enumerate(NA_VARIANT_TILES):
        r0, b0 = _na_tile_band(t)
        for kr in range(NA_TILE_BAND):
            for p in range(NA_TILE_ROWS // 2):
                pair = []
                for r in (r0 + 2 * p, r0 + 2 * p + 1):
                    rs = min(max(r - NA_WIN_ROWS // 2, 0), GRID_ROWS - NA_WIN_ROWS)
                    pair.append(b0 + kr - r + NA_WIN_ROWS - 1 if rs <= b0 + kr < rs + NA_WIN_ROWS else None)
                if tuple(pair) not in pairs:
                    pairs.append(tuple(pair))
                plan[variant, kr, p] = pairs.index(tuple(pair))
    return pairs, plan


def _na_bias(rpb):
    cols = np.arange(GRID_W)
    col_start = np.clip(cols - NA_WIN_COLS // 2, 0, GRID_W - NA_WIN_COLS)
    col_in = (cols[:, None] >= col_start[None, :]) & (cols[:, None] < col_start[None, :] + NA_WIN_COLS)
    blocks = jnp.where(col_in, _toeplitz(rpb) * LOG2E, NEG_INF)
    outside = jnp.full((NA_HEADS, GRID_W, GRID_W), NEG_INF, F32)
    half = lambda x: outside if x is None else blocks[:, x]
    pairs, _ = _na_bias_plan()
    tab = jnp.stack([jnp.concatenate([half(x0), half(x1)], axis=-1) for x0, x1 in pairs], axis=1)
    return tab.reshape(NA_HEADS // 2, 2, len(pairs), GRID_W, LANES).astype(F32)


def _na_attention(qkv, bias):
    w = NA_PAIRS_PER_STEP * LANES
    n_hb = D // w
    kb, vb = D // w, 2 * D // w
    in_specs = [pl.BlockSpec((SEQ, w), lambda h, b: (b, h)),
                pl.BlockSpec((SEQ, w), lambda h, b: (b, kb + h)),
                pl.BlockSpec((SEQ, w), lambda h, b: (b, vb + h)),
                pl.BlockSpec((CTX, w), lambda h, b: (N_LAT // CTX + b, kb + h)),
                pl.BlockSpec((CTX, w), lambda h, b: (N_LAT // CTX + b, vb + h)),
                pl.BlockSpec((NA_PAIRS_PER_STEP,) + bias.shape[1:], lambda h, b: (h, 0, 0, 0, 0))]
    return pl.pallas_call(
        _na_attn_kernel,
        grid=(n_hb, BATCH),
        in_specs=in_specs,
        out_specs=pl.BlockSpec((SEQ, w), lambda h, b: (b, h)),
        out_shape=jax.ShapeDtypeStruct((N_LAT, D), BF16),
        scratch_shapes=[pltpu.VMEM((2, NA_TILE_BAND * GRID_W + CTX, CHAIN_N), F32)],
        compiler_params=_params("arbitrary", "arbitrary"),
        name="na_attention",
    )(qkv, qkv, qkv, qkv, qkv, bias)


def _dup_heads(w, n_heads):
    w = w.reshape(D, n_heads, 1, HEAD)
    return jnp.broadcast_to(w, (D, n_heads, LANES // HEAD, HEAD)).reshape(D, n_heads * LANES)


def _pad_heads(w, width):
    k = w.shape[0]
    w = w.reshape(k, MLA_HEADS, width)
    return jnp.pad(w, ((0, 0), (0, 0), (0, LANES - width))).reshape(k, MLA_HEADS * LANES)


def kernel(x, c, ctx, c_ctx, w_mod, b_mod, norm_g, w_ffn_in, w_ffn_out, da_w_qkv, da_lam_q1, da_lam_k1, da_lam_q2, da_lam_k2, da_subln_g, da_w_o, gqa_w_qkv, gqa_q_norm_g, gqa_k_norm_g, gqa_w_o, mla_w_down, mla_q_norm_g, mla_kv_norm_g, mla_w_uq, mla_w_ukv, mla_w_o, na_w_qkv, na_rpb, na_w_o, final_g):
    assert x.shape == (BATCH, SEQ, D) and ctx.shape == (BATCH, CTX, D) and w_mod.shape[0] == DEPTH == 4

    cvec = jnp.zeros((MOD_ROWS, D), F32).at[:BATCH].set(c).at[BATCH].set(c_ctx)
    mod = _modulation(cvec, w_mod, b_mod).reshape(DEPTH, MOD_ROWS, N_MOD, D)
    xs = (x.reshape(N_LAT, D), ctx.reshape(BATCH * CTX, D))
    w_ffn = (w_ffn_in[0, 0].astype(BF16), w_ffn_out[0, 0].astype(BF16))

    rope64 = _rope_tables(HEAD, 0, HEAD)
    rope_mla = _rope_tables(MLA_ROPE, MLA_NOPE, LANES)
    ones_blk = jnp.kron(jnp.eye(MXU_N // HEAD, dtype=F32), jnp.ones((HEAD, HEAD), F32)).astype(BF16)
    blk_spec = _resident((MXU_N, MXU_N), lambda i: (0, 0))

    for layer in range(DEPTH):
        last = layer == DEPTH - 1
        xs, w_ffn = _ffn(xs, mod, layer, 0, norm_g[layer, 0], *w_ffn, n_rows=N_TOK,
                         next_w=(w_ffn_in, w_ffn_out, layer, 1))
        g_mix = norm_g[layer, 1]
        if layer == 0:
            qkv = _project(_da_proj_kernel, xs, mod, layer, g_mix, da_w_qkv[0].astype(BF16),
                           list(rope64), _rope_specs(), 3 * D, "da_proj")
            lam_init = 0.8 - 0.6 * math.exp(-0.3 * layer)
            lam = jnp.zeros((8, LANES), F32).at[:4, :HEAD].set(
                jnp.stack([da_lam_q1[0], da_lam_k1[0], da_lam_q2[0], da_lam_k2[0]]))
            const3 = lambda *_: (0, 0)
            o = _attention(functools.partial(_da_chains, lam_init=lam_init), qkv,
                           LANES, LANES, LANES, LANES, D, 2 * D, DA_HEADS,
                           extra=(lam, _row(da_subln_g[0])),
                           extra_specs=(pl.BlockSpec((8, LANES), const3), pl.BlockSpec((1, LANES), const3)),
                           name="da_attention")
            w_o = da_w_o[0]
        elif layer == 1:
            wq, wk, wv = jnp.split(gqa_w_qkv[0], [GQA_HEADS * HEAD, (GQA_HEADS + GQA_KV) * HEAD], axis=1)
            w = jnp.concatenate([wq, _dup_heads(wk, GQA_KV), _dup_heads(wv, GQA_KV)], axis=1).astype(BF16)
            gains = (_row(jnp.tile(gqa_q_norm_g[0], MXU_N // HEAD)), _row(jnp.tile(gqa_k_norm_g[0], MXU_N // HEAD)))
            qkv = _project(_gqa_proj_kernel, xs, mod, layer, g_mix, w,
                           list(rope64) + list(gains) + [ones_blk],
                           _rope_specs() + [_row_spec(MXU_N)] * 2 + [blk_spec],
                           D + 2 * GQA_KV * LANES, "gqa_proj")
            o = _attention(functools.partial(_pair_chains, wide_keys=False), qkv,
                           2 * LANES, LANES, LANES, 2 * LANES, D, D + GQA_KV * LANES, GQA_KV,
                           name="gqa_attention")
            w_o = gqa_w_o[0]
        elif layer == 2:
            wd = mla_w_down[0]
            n_lora = MLA_Q_LORA + MLA_KV_LORA
            wd = jnp.concatenate([wd[:, :n_lora], jnp.zeros((D, MLA_NOPE), F32), wd[:, n_lora:],
                                  jnp.zeros((D, LANES - MLA_NOPE - MLA_ROPE), F32)], axis=1).astype(BF16)
            wuq = _pad_heads(mla_w_uq[0], MLA_NOPE + MLA_ROPE).astype(BF16)
            wukv = mla_w_ukv[0].reshape(MLA_KV_LORA, MLA_HEADS, 2 * HEAD)
            wuk = _pad_heads(wukv[:, :, :MLA_NOPE].reshape(MLA_KV_LORA, -1), MLA_NOPE)
            wuv = wukv[:, :, MLA_NOPE:].reshape(MLA_KV_LORA, MLA_HEADS * HEAD)
            wukv = jnp.concatenate([wuk, wuv], axis=1).astype(BF16)
            n_q = MLA_HEADS * LANES
            const = lambda i: (0, 0)
            qkv = _project(_mla_proj_kernel, xs, mod, layer, g_mix, wd,
                           list(rope_mla) + [_row(mla_q_norm_g[0]), _row(mla_kv_norm_g[0]), wuq, wukv],
                           _rope_specs() + [_row_spec(MLA_Q_LORA), _row_spec(MLA_KV_LORA),
                                            _resident(wuq.shape, const), _resident(wukv.shape, const)],
                           2 * n_q + D, "mla_proj")
            o = _attention(functools.partial(_pair_chains, wide_keys=True), qkv,
                           2 * LANES, 2 * LANES, LANES, LANES, n_q, 2 * n_q, MLA_HEADS // 2,
                           name="mla_attention")
            w_o = mla_w_o[0]
        else:
            qkv = _project(_na_proj_kernel, xs, mod, layer, g_mix, na_w_qkv[0].astype(BF16), [], [], 3 * D, "na_proj")
            o = _na_attention(qkv, _na_bias(na_rpb[0]))
            w_o = na_w_o[0]
        xs, w_ffn = _ffn(xs, mod, layer, 1, norm_g[layer, 2], *w_ffn, n_rows=N_LAT if last else N_TOK,
                         attn=(o, w_o.astype(BF16)), final_g=final_g if last else None,
                         next_w=None if last else (w_ffn_in, w_ffn_out, layer + 1, 0))
    return xs.reshape(BATCH, SEQ, D)
```
